```python
import math
import jax, jax.numpy as jnp
from jax import lax
import numpy as np

D_MODEL = 1024
BATCH = 2
SEQ = 8192
DEPTH = 4

N_A = DEPTH // 2
N_B = DEPTH - N_A
GDN_HEADS = 8
GDN_HEAD_DIM = D_MODEL // GDN_HEADS
GDN_CONV = 4
GDN_CHUNK = 64
SB_HEADS = 8
SB_HEAD_DIM = D_MODEL // SB_HEADS
SB_BLOCK = 128
D_FF = ((8 * D_MODEL // 3 + 127) // 128) * 128
FFN_CONV = 3
EPS = 1e-6

kernel_name = "yoco_gdn_stickbreaking_convffn"


def rmsnorm(x, g):
    xf = x.astype(jnp.float32)
    xf = xf * lax.rsqrt(jnp.mean(xf * xf, axis=-1, keepdims=True) + EPS)
    return (xf * g.astype(jnp.float32)).astype(x.dtype)


def l2norm(x):
    xf = x.astype(jnp.float32)
    return xf * lax.rsqrt(jnp.sum(xf * xf, axis=-1, keepdims=True) + EPS)


def causal_dwconv(x, w):
    K, C = w.shape
    return lax.conv_general_dilated(
        x, w[:, None, :].astype(x.dtype), window_strides=(1,), padding=[(K - 1, 0)],
        dimension_numbers=("NWC", "WIO", "NWC"), feature_group_count=C)


def gated_delta_rule(q, k, v, g, beta):
    B, T, H, Dk = q.shape
    Dv = v.shape[-1]
    C = GDN_CHUNK
    N = T // C
    f32 = jnp.float32

    def chunks(t):
        t = t.astype(f32).reshape((B, N, C, H) + t.shape[3:])
        return jnp.moveaxis(t, 3, 1)

    q, k, v, beta = chunks(q), chunks(k), chunks(v), chunks(beta)
    g = jnp.cumsum(chunks(g), axis=-1)
    tri = jnp.tril(jnp.ones((C, C), dtype=bool))
    strict = jnp.tril(jnp.ones((C, C), dtype=bool), -1)
    decay = jnp.exp(jnp.where(tri, g[..., :, None] - g[..., None, :], -jnp.inf))
    k_beta = k * beta[..., None]
    L = jnp.where(strict, jnp.einsum("bhnid,bhnjd->bhnij", k_beta, k) * decay, 0.0)
    rhs = jnp.concatenate([v * beta[..., None], k_beta * jnp.exp(g)[..., None]], axis=-1)
    sol = lax.linalg.triangular_solve(L, rhs, left_side=True, lower=True, unit_diagonal=True)
    u, w = sol[..., :Dv], sol[..., Dv:]
    qk = jnp.where(tri, jnp.einsum("bhnid,bhnjd->bhnij", q, k) * decay, 0.0)
    g_last = g[..., -1]
    q_dec = q * jnp.exp(g)[..., None]
    k_dec = k * jnp.exp(g_last[..., None] - g)[..., None]

    def step(S, inp):
        qk_i, q_dec_i, k_dec_i, u_i, w_i, gl_i = inp
        v_new = u_i - jnp.einsum("bhck,bhkv->bhcv", w_i, S)
        o = jnp.einsum("bhck,bhkv->bhcv", q_dec_i, S) + jnp.einsum("bhij,bhjv->bhiv", qk_i, v_new)
        S = S * jnp.exp(gl_i)[..., None, None] + jnp.einsum("bhck,bhcv->bhkv", k_dec_i, v_new)
        return S, o

    xs = tuple(jnp.moveaxis(t, 2, 0) for t in (qk, q_dec, k_dec, u, w, g_last))
    S0 = jnp.zeros((B, H, Dk, Dv), f32)
    _, o = lax.scan(step, S0, xs)
    return jnp.transpose(o, (1, 0, 3, 2, 4)).reshape(B, T, H, Dv)


def gdn_mixer(h, w_in, conv_w, a_log, dt_bias, o_gain, w_out):
    B, T, D = h.shape
    H, Dh = GDN_HEADS, GDN_HEAD_DIM
    proj = h @ w_in
    qkv, z, a, b = jnp.split(proj, [3 * D, 4 * D, 4 * D + H], axis=-1)
    qkv = jax.nn.silu(causal_dwconv(qkv, conv_w))
    q, k, v = jnp.split(qkv, 3, axis=-1)
    q = l2norm(q.reshape(B, T, H, Dh)) * (Dh ** -0.5)
    k = l2norm(k.reshape(B, T, H, Dh))
    v = v.reshape(B, T, H, Dh)
    g = -jnp.exp(a_log.astype(jnp.float32)) * jax.nn.softplus(
        a.astype(jnp.float32) + dt_bias.astype(jnp.float32))
    beta = jax.nn.sigmoid(b.astype(jnp.float32))
    o = gated_delta_rule(q, k, v, g, beta)
    o = rmsnorm(o, o_gain) * jax.nn.silu(z.reshape(B, T, H, Dh).astype(jnp.float32))
    return o.reshape(B, T, D).astype(h.dtype) @ w_out


def stick_breaking_attention(q, k, v):
    B, H, T, D = q.shape
    scale = D ** -0.5
    outs = []
    for blk in range(T // SB_BLOCK):
        lo, hi = blk * SB_BLOCK, (blk + 1) * SB_BLOCK
        z = jnp.einsum("bhtd,bhsd->bhts", q[:, :, lo:hi], k[:, :, :hi],
                       preferred_element_type=jnp.float32) * scale
        t_idx = lo + jnp.arange(SB_BLOCK)[:, None]
        s_idx = jnp.arange(hi)[None, :]
        causal = s_idx < t_idx
        log_fail = jnp.where(causal, jax.nn.log_sigmoid(-z), 0.0)
        after = lax.cumsum(log_fail, axis=3, reverse=True) - log_fail
        a = jnp.where(causal, jnp.exp(jax.nn.log_sigmoid(z) + after), 0.0)
        outs.append(jnp.einsum("bhts,bhsd->bhtd", a, v[:, :, :hi].astype(jnp.float32)))
    return jnp.concatenate(outs, axis=2).astype(q.dtype)


def conv_ffn(h, w_up, conv_w, w_down):
    u = causal_dwconv(h @ w_up, conv_w)
    gate, up = jnp.split(u, 2, axis=-1)
    return (jax.nn.silu(gate) * up) @ w_down


def setup_inputs(seed: int = 0) -> dict:
    key = jax.random.key(seed)
    ks = jax.random.split(key, 20)
    D, H = D_MODEL, GDN_HEADS
    f32 = jnp.float32
    out_scale = (2 * DEPTH) ** -0.5

    def nrm(k, shape, fan_in, gain=1.0):
        return jax.random.normal(k, shape, f32) * (gain * fan_in ** -0.5)

    def gvec(k, shape):
        return 1.0 + 0.02 * jax.random.normal(k, shape, f32)

    dt = jnp.exp(jax.random.uniform(ks[4], (N_A, H), f32, math.log(1e-3), math.log(1e-1)))
    return {
        "x": jax.random.normal(ks[0], (BATCH, SEQ, D), f32),
        "a_norm": gvec(ks[1], (N_A, D)),
        "a_w_in": nrm(ks[2], (N_A, D, 4 * D + 2 * H), D),
        "a_conv": nrm(ks[3], (N_A, GDN_CONV, 3 * D), GDN_CONV),
        "a_log": jnp.log(jax.random.uniform(ks[5], (N_A, H), f32, 1.0, 16.0)),
        "a_dt_bias": dt + jnp.log(-jnp.expm1(-dt)),
        "a_out_norm": gvec(ks[6], (N_A, GDN_HEAD_DIM)),
        "a_w_out": nrm(ks[7], (N_A, D, D), D, out_scale),
        "kv_norm": gvec(ks[8], (D,)),
        "w_kv": nrm(ks[9], (D, 2 * D), D),
        "k_norm": gvec(ks[10], (SB_HEAD_DIM,)),
        "b_norm": gvec(ks[11], (N_B, D)),
        "b_w_q": nrm(ks[12], (N_B, D, D), D),
        "q_norm": gvec(ks[13], (N_B, SB_HEAD_DIM)),
        "b_w_out": nrm(ks[14], (N_B, D, D), D, out_scale),
        "ffn_norm": gvec(ks[15], (DEPTH, D)),
        "ffn_w_up": nrm(ks[16], (DEPTH, D, 2 * D_FF), D),
        "ffn_conv": nrm(ks[17], (DEPTH, FFN_CONV, 2 * D_FF), FFN_CONV),
        "ffn_w_down": nrm(ks[18], (DEPTH, D_FF, D), D_FF, out_scale),
    }


def reference(x, a_norm, a_w_in, a_conv, a_log, a_dt_bias, a_out_norm, a_w_out,
              kv_norm, w_kv, k_norm, b_norm, b_w_q, q_norm, b_w_out,
              ffn_norm, ffn_w_up, ffn_conv, ffn_w_down):
    B, T, D = x.shape
    H, Dh = SB_HEADS, SB_HEAD_DIM
    for layer in range(DEPTH):
        if layer < N_A:
            x = x + gdn_mixer(rmsnorm(x, a_norm[layer]), a_w_in[layer], a_conv[layer],
                              a_log[layer], a_dt_bias[layer], a_out_norm[layer], a_w_out[layer])
        else:
            j = layer - N_A
            if j == 0:
                kv = rmsnorm(x, kv_norm) @ w_kv
                k_s, v_s = jnp.split(kv, 2, axis=-1)
                k_s = jnp.transpose(rmsnorm(k_s.reshape(B, T, H, Dh), k_norm), (0, 2, 1, 3))
                v_s = jnp.transpose(v_s.reshape(B, T, H, Dh), (0, 2, 1, 3))
            q = rmsnorm((rmsnorm(x, b_norm[j]) @ b_w_q[j]).reshape(B, T, H, Dh), q_norm[j])
            o = stick_breaking_attention(jnp.transpose(q, (0, 2, 1, 3)), k_s, v_s)
            x = x + jnp.transpose(o, (0, 2, 1, 3)).reshape(B, T, D) @ b_w_out[j]
        x = x + conv_ffn(rmsnorm(x, ffn_norm[layer]), ffn_w_up[layer], ffn_conv[layer], ffn_w_down[layer])
    return x
```

```python
import functools

import jax
import jax.numpy as jnp
from jax import lax
from jax.experimental import pallas as pl
from jax.experimental.pallas import tpu as pltpu

EPS = 1e-6
HEAD_DIM = 128
GDN_CHUNK = 64
FFN_CONV_W = 3
F32_SUBLANES = 8
BF16_SUBLANES = 16
SB_KEY_BLOCK = 128
SB_STICK_EXHAUSTED = 104.0
VMEM_LIMIT_BYTES = 56 * 1024 * 1024

_f32 = jnp.float32
_bf16 = jnp.bfloat16


def _params(*semantics):
    return pltpu.CompilerParams(dimension_semantics=semantics, vmem_limit_bytes=VMEM_LIMIT_BYTES)


def _rms(x, gain):
    ms = jnp.mean(x * x, axis=-1, keepdims=True)
    return (x * lax.rsqrt(ms + EPS)) * gain


def _softplus(x):
    return jnp.maximum(x, 0.0) + jnp.log(1.0 + jnp.exp(-jnp.abs(x)))


def _sigmoid(x):
    return 1.0 / (1.0 + jnp.exp(-x))


def _dot(a, b):
    return jnp.dot(a, b, preferred_element_type=_f32)


def _dot_nt(a, b):
    return lax.dot_general(a, b, (((1,), (1,)), ((), ())), preferred_element_type=_f32)


def _dot_tn(a, b):
    return lax.dot_general(a, b, (((0,), (0,)), ((), ())), preferred_element_type=_f32)


def _dot_f32(a, b):
    return jnp.dot(a, b, preferred_element_type=_f32, precision=lax.Precision.HIGHEST)


def _split_bf16(x, pieces):
    out = []
    for _ in range(pieces):
        hi = x.astype(_bf16)
        out.append(hi)
        x = x - hi.astype(_f32)
    return out


def _norm_matmul_kernel(x_ref, g_ref, w_ref, *rest, head_norm):
    if head_norm:
        hg_ref, o_ref, xn_ref = rest
    else:
        o_ref, xn_ref = rest

    @pl.when(pl.program_id(1) == 0)
    def _():
        xn_ref[...] = _rms(x_ref[...], g_ref[...]).astype(xn_ref.dtype)

    y = _dot(xn_ref[...], w_ref[...])
    if head_norm:
        parts = []
        for h in range(y.shape[1] // HEAD_DIM):
            parts.append(_rms(y[:, h * HEAD_DIM:(h + 1) * HEAD_DIM], hg_ref[...]))
        y = jnp.concatenate(parts, axis=1)
    o_ref[...] = y.astype(o_ref.dtype)


def norm_matmul(x, gain, w, out_dtype, head_gain=None, tm=1024, tn=1024):
    n, d = x.shape
    nout = w.shape[1]
    tm, tn = min(tm, n), min(tn, nout)
    in_specs = [pl.BlockSpec((tm, d), lambda i, j: (i, 0)),
                pl.BlockSpec((1, d), lambda i, j: (0, 0)),
                pl.BlockSpec((d, tn), lambda i, j: (0, j))]
    args = [x, gain.reshape(1, d), w]
    if head_gain is not None:
        in_specs.append(pl.BlockSpec((1, HEAD_DIM), lambda i, j: (0, 0)))
        args.append(head_gain.reshape(1, HEAD_DIM))
    return pl.pallas_call(
        functools.partial(_norm_matmul_kernel, head_norm=head_gain is not None),
        grid=(n // tm, nout // tn),
        in_specs=in_specs,
        out_specs=pl.BlockSpec((tm, tn), lambda i, j: (i, j)),
        out_shape=jax.ShapeDtypeStruct((n, nout), out_dtype),
        scratch_shapes=[pltpu.VMEM((tm, d), _bf16)],
        compiler_params=_params("parallel", "arbitrary"),
        name="norm_matmul",
    )(*args)


def _matmul_residual_kernel(a_ref, w_ref, x_ref, o_ref):
    o_ref[...] = x_ref[...] + _dot(a_ref[...], w_ref[...])


def matmul_residual(a, w, x, tm=1024):
    n, k = a.shape
    d = w.shape[1]
    tm = min(tm, n)
    return pl.pallas_call(
        _matmul_residual_kernel,
        grid=(n // tm,),
        in_specs=[pl.BlockSpec((tm, k), lambda i: (i, 0)),
                  pl.BlockSpec((k, d), lambda i: (0, 0)),
                  pl.BlockSpec((tm, d), lambda i: (i, 0))],
        out_specs=pl.BlockSpec((tm, d), lambda i: (i, 0)),
        out_shape=jax.ShapeDtypeStruct((n, d), _f32),
        compiler_params=_params("parallel"),
        name="matmul_residual",
    )(a, w, x)


def _gdn_gate_kernel(x_ref, g_ref, w_ref, alog_ref, dtb_ref, o_ref, *, heads):
    tm = x_ref.shape[0]
    xn = _rms(x_ref[...], g_ref[...]).astype(_bf16)
    y = _dot(xn, w_ref[...])
    g = -jnp.exp(alog_ref[...]) * _softplus(y + dtb_ref[...])
    beta = _sigmoid(y)
    r = lax.broadcasted_iota(jnp.int32, (tm, tm), 0)
    c = lax.broadcasted_iota(jnp.int32, (tm, tm), 1)
    tri = ((r // GDN_CHUNK == c // GDN_CHUNK) & (r >= c)).astype(_bf16)
    gc = sum(_dot(tri, piece) for piece in _split_bf16(g, 3))
    lane = lax.broadcasted_iota(jnp.int32, y.shape, 1)
    o_ref[...] = jnp.where(lane < heads, gc, beta)


def gdn_gates(x, gain, w_ab, a_log, dt_bias, tm=512):
    n, d = x.shape
    heads = a_log.shape[0]
    tm = min(tm, n)
    pad = lambda v: jnp.zeros((1, HEAD_DIM), _f32).at[0, :heads].set(v.astype(_f32))
    w = jnp.zeros((d, HEAD_DIM), _bf16).at[:, :2 * heads].set(w_ab.astype(_bf16))
    return pl.pallas_call(
        functools.partial(_gdn_gate_kernel, heads=heads),
        grid=(n // tm,),
        in_specs=[pl.BlockSpec((tm, d), lambda i: (i, 0)),
                  pl.BlockSpec((1, d), lambda i: (0, 0)),
                  pl.BlockSpec((d, HEAD_DIM), lambda i: (0, 0)),
                  pl.BlockSpec((1, HEAD_DIM), lambda i: (0, 0)),
                  pl.BlockSpec((1, HEAD_DIM), lambda i: (0, 0))],
        out_specs=pl.BlockSpec((tm, HEAD_DIM), lambda i: (i, 0)),
        out_shape=jax.ShapeDtypeStruct((n, HEAD_DIM), _f32),
        compiler_params=_params("parallel"),
        name="gdn_gates",
    )(x, gain.reshape(1, d), w, pad(a_log), pad(dt_bias))


def _unit_lower_inverse(low, eye):
    x = eye - low
    p = low
    for _ in range(5):
        p = _dot_f32(p, p)
        x = x + _dot_f32(x, p)
    return x


def _gdn_core_kernel(q_ref, k_ref, v_ref, z_ref, gb_ref, cq_ref, ck_ref, cv_ref, gain_ref, o_ref,
                     halo_ref, s_ref, q_s, k_s, v_s, g_s, b_s, o_s, *, heads):
    tb = q_ref.shape[0]
    hd = HEAD_DIM
    cl = GDN_CHUNK
    h = pl.program_id(1)

    @pl.when(pl.program_id(2) == 0)
    def _():
        halo_ref[...] = jnp.zeros_like(halo_ref)
        s_ref[...] = jnp.zeros_like(s_ref)

    def conv_silu(x_ref, w_ref, idx):
        raw = x_ref[...]
        ext = jnp.concatenate([halo_ref[idx], raw], axis=0)
        w = w_ref[...]
        y = w[3:4] * raw
        for d in range(1, 4):
            y = y + w[3 - d:4 - d] * ext[F32_SUBLANES - d:F32_SUBLANES - d + tb]
        halo_ref[idx] = raw[tb - F32_SUBLANES:]
        return y * _sigmoid(y)

    def l2n(x):
        return x * lax.rsqrt(jnp.sum(x * x, axis=-1, keepdims=True) + EPS)

    q_s[...] = l2n(conv_silu(q_ref, cq_ref, 0)) * (hd ** -0.5)
    k_s[...] = l2n(conv_silu(k_ref, ck_ref, 1))
    v_s[...] = conv_silu(v_ref, cv_ref, 2)

    gb = gb_ref[...]
    lane = lax.broadcasted_iota(jnp.int32, gb.shape, 1)
    gcol = jnp.sum(jnp.where(lane == h, gb, 0.0), axis=-1, keepdims=True)
    bcol = jnp.sum(jnp.where(lane == h + heads, gb, 0.0), axis=-1, keepdims=True)
    g_s[...] = jnp.broadcast_to(gcol, (tb, hd))
    b_s[...] = jnp.broadcast_to(bcol, (tb, hd))

    ri = lax.broadcasted_iota(jnp.int32, (cl, cl), 0)
    ci = lax.broadcasted_iota(jnp.int32, (cl, cl), 1)
    eye_mask = ri == ci
    tril = ri >= ci
    strict = ri > ci
    eye = eye_mask.astype(_f32)

    def chunk(c, carry):
        r = pl.ds(pl.multiple_of(c * cl, cl), cl)
        q, k, v, gc, bt = q_s[r, :], k_s[r, :], v_s[r, :], g_s[r, :], b_s[r, :]
        gc_sq = gc[:, :cl]
        grow = jnp.sum(jnp.where(eye_mask, gc_sq, 0.0), axis=0, keepdims=True)
        decay = jnp.where(tril, jnp.exp(gc_sq - grow), 0.0)
        eg = jnp.exp(gc)
        kb = k * bt
        k16 = k.astype(_bf16)
        low = jnp.where(strict, _dot_nt(kb.astype(_bf16), k16) * decay, 0.0)
        tinv = _unit_lower_inverse(low, eye)
        rhs = jnp.concatenate([v * bt, kb * eg], axis=1)
        sol = _dot_f32(tinv, rhs)
        u, w = sol[:, :hd], sol[:, hd:]
        qk = jnp.where(tril, _dot_nt(q.astype(_bf16), k16) * decay, 0.0)
        s = s_ref[...]
        s16 = s.astype(_bf16)
        v_new = u - _dot(w.astype(_bf16), s16)
        vn16 = v_new.astype(_bf16)
        o_s[r, :] = _dot((q * eg).astype(_bf16), s16) + _dot(qk.astype(_bf16), vn16)
        gl = gc[cl - 1:cl, :]
        k_dec = k * jnp.exp(gl - gc)
        s_ref[...] = s * jnp.exp(gl) + _dot_tn(k_dec.astype(_bf16), vn16)
        return carry

    lax.fori_loop(0, tb // cl, chunk, 0)

    z = z_ref[...]
    o_ref[...] = (_rms(o_s[...], gain_ref[...]) * (z * _sigmoid(z))).astype(o_ref.dtype)


def gdn_core(proj, gb, conv_w, out_gain, batch, seq, heads, tb=512):
    n = proj.shape[0]
    tb = min(tb, seq)
    nt = seq // tb
    row = lambda b, h, t: b * nt + t
    col = lambda off: (lambda b, h, t: (row(b, h, t), off * heads + h))
    ccol = lambda off: (lambda b, h, t: (0, off * heads + h))
    blk = (tb, HEAD_DIM)
    scr = lambda: pltpu.VMEM(blk, _f32)
    return pl.pallas_call(
        functools.partial(_gdn_core_kernel, heads=heads),
        grid=(batch, heads, nt),
        in_specs=[pl.BlockSpec(blk, col(0)), pl.BlockSpec(blk, col(1)), pl.BlockSpec(blk, col(2)),
                  pl.BlockSpec(blk, col(3)),
                  pl.BlockSpec(blk, lambda b, h, t: (row(b, h, t), 0)),
                  pl.BlockSpec((4, HEAD_DIM), ccol(0)), pl.BlockSpec((4, HEAD_DIM), ccol(1)),
                  pl.BlockSpec((4, HEAD_DIM), ccol(2)),
                  pl.BlockSpec((1, HEAD_DIM), lambda b, h, t: (0, 0))],
        out_specs=pl.BlockSpec(blk, lambda b, h, t: (row(b, h, t), h)),
        out_shape=jax.ShapeDtypeStruct((n, heads * HEAD_DIM), _bf16),
        scratch_shapes=[pltpu.VMEM((3, F32_SUBLANES, HEAD_DIM), _f32),
                        pltpu.VMEM((HEAD_DIM, HEAD_DIM), _f32),
                        scr(), scr(), scr(), scr(), scr(), scr()],
        compiler_params=_params("parallel", "parallel", "arbitrary"),
        name="gdn_core",
    )(proj, proj, proj, proj, gb, conv_w, conv_w, conv_w, out_gain.reshape(1, HEAD_DIM))


def _ffn_kernel(x_ref, xh_ref, g_ref, wg_ref, wu_ref, cg_ref, cu_ref, wd_ref, o_ref, xn_ref, acc_ref,
                *, blocks_per_seq):
    tm = x_ref.shape[0]
    halo = BF16_SUBLANES
    i, f = pl.program_id(0), pl.program_id(1)

    @pl.when(f == 0)
    def _():
        xn_ref[halo:, :] = _rms(x_ref[...], g_ref[...]).astype(xn_ref.dtype)
        prev = jnp.where(i % blocks_per_seq == 0, 0.0, _rms(xh_ref[...], g_ref[...]))
        xn_ref[:halo, :] = prev.astype(xn_ref.dtype)
        acc_ref[...] = jnp.zeros_like(acc_ref)

    xn = xn_ref[...]

    def conv_branch(w_ref, c_ref):
        u = _dot(xn, w_ref[...])
        cw = c_ref[...]
        return (cw[2:3] * u[halo:] + cw[1:2] * u[halo - 1:halo - 1 + tm]
                + cw[0:1] * u[halo - 2:halo - 2 + tm])

    gate = conv_branch(wg_ref, cg_ref)
    up = conv_branch(wu_ref, cu_ref)
    act = ((gate * _sigmoid(gate)) * up).astype(_bf16)
    acc_ref[...] += _dot(act, wd_ref[...])

    @pl.when(f == pl.num_programs(1) - 1)
    def _():
        o_ref[...] = x_ref[...] + acc_ref[...]


def conv_ffn(x, gain, w_up, conv_w, w_down, seq, tm=1024, tf=256):
    n, d = x.shape
    ff = w_down.shape[0]
    tm, tf = min(tm, seq), min(tf, ff)
    nf = ff // tf
    hb = tm // BF16_SUBLANES
    return pl.pallas_call(
        functools.partial(_ffn_kernel, blocks_per_seq=seq // tm),
        grid=(n // tm, nf),
        in_specs=[pl.BlockSpec((tm, d), lambda i, f: (i, 0)),
                  pl.BlockSpec((BF16_SUBLANES, d), lambda i, f: (jnp.maximum(i * hb - 1, 0), 0)),
                  pl.BlockSpec((1, d), lambda i, f: (0, 0)),
                  pl.BlockSpec((d, tf), lambda i, f: (0, f)),
                  pl.BlockSpec((d, tf), lambda i, f: (0, nf + f)),
                  pl.BlockSpec((FFN_CONV_W, tf), lambda i, f: (0, f)),
                  pl.BlockSpec((FFN_CONV_W, tf), lambda i, f: (0, nf + f)),
                  pl.BlockSpec((tf, d), lambda i, f: (f, 0))],
        out_specs=pl.BlockSpec((tm, d), lambda i, f: (i, 0)),
        out_shape=jax.ShapeDtypeStruct((n, d), _f32),
        scratch_shapes=[pltpu.VMEM((BF16_SUBLANES + tm, d), _bf16), pltpu.VMEM((tm, d), _f32)],
        compiler_params=_params("parallel", "arbitrary"),
        name="conv_ffn",
    )(x, x, gain.reshape(1, d), w_up, w_up, conv_w, conv_w, w_down)


def _sb_kernel(q_ref, k_ref, v_ref, o_ref, c_ref, acc_ref, *, scale):
    tq = q_ref.shape[0]
    tk = SB_KEY_BLOCK
    nsub = tq // tk
    i = pl.program_id(2)
    rr = lax.broadcasted_iota(jnp.int32, (tk, tk), 0)
    cc = lax.broadcasted_iota(jnp.int32, (tk, tk), 1)
    tri = (rr >= cc).astype(_bf16)
    c_ref[...] = jnp.zeros_like(c_ref)
    acc_ref[...] = jnp.zeros_like(acc_ref)

    def process(row0, kstart, masked):
        rows = slice(row0, tq)
        kb = k_ref[pl.ds(kstart, tk), :]
        vb = v_ref[pl.ds(kstart, tk), :]
        z = _dot_nt(q_ref[rows, :], kb) * scale
        sp = _softplus(z)
        if masked:
            t_loc = lax.broadcasted_iota(jnp.int32, z.shape, 0)
            s_loc = lax.broadcasted_iota(jnp.int32, z.shape, 1)
            mask = s_loc < t_loc
            sp = jnp.where(mask, sp, 0.0)
        cum = sum(_dot(piece, tri) for piece in _split_bf16(sp, 2))
        c = c_ref[rows, :]
        a = jnp.exp(z - cum - c)
        if masked:
            a = jnp.where(mask, a, 0.0)
        acc_ref[rows, :] += _dot(a.astype(_bf16), vb)
        c_ref[rows, :] = c + cum[:, 0:1]

    for sub in reversed(range(nsub)):
        process(sub * tk, pl.multiple_of(i * tq + sub * tk, tk), True)

    def cond(carry):
        j, go = carry
        return jnp.logical_and(j >= 0, go > 0)

    def body(carry):
        j, _ = carry
        process(0, pl.multiple_of(j * tk, tk), False)
        go = (jnp.min(c_ref[...]) <= SB_STICK_EXHAUSTED).astype(jnp.int32)
        return j - 1, go

    lax.while_loop(cond, body, (i * nsub - 1, jnp.int32(1)))
    o_ref[...] = acc_ref[...].astype(o_ref.dtype)


def sb_attention(q, k, v, batch, seq, tq=256):
    n, d = q.shape
    heads = d // HEAD_DIM
    tq = min(tq, seq)
    nq = seq // tq
    return pl.pallas_call(
        functools.partial(_sb_kernel, scale=HEAD_DIM ** -0.5),
        grid=(batch, heads, nq),
        in_specs=[pl.BlockSpec((tq, HEAD_DIM), lambda b, h, i: (b * nq + i, h)),
                  pl.BlockSpec((seq, HEAD_DIM), lambda b, h, i: (b, h)),
                  pl.BlockSpec((seq, HEAD_DIM), lambda b, h, i: (b, h))],
        out_specs=pl.BlockSpec((tq, HEAD_DIM), lambda b, h, i: (b * nq + i, h)),
        out_shape=jax.ShapeDtypeStruct((n, d), _bf16),
        scratch_shapes=[pltpu.VMEM((tq, HEAD_DIM), _f32), pltpu.VMEM((tq, HEAD_DIM), _f32)],
        compiler_params=_params("parallel", "parallel", "arbitrary"),
        name="sb_attention",
    )(q, k, v)


def kernel(x, a_norm, a_w_in, a_conv, a_log, a_dt_bias, a_out_norm, a_w_out, kv_norm, w_kv, k_norm, b_norm, b_w_q, q_norm, b_w_out, ffn_norm, ffn_w_up, ffn_conv, ffn_w_down):
    batch, seq, d = x.shape
    n = batch * seq
    n_a = a_w_in.shape[0]
    n_b = b_w_q.shape[0]
    gdn_heads = a_log.shape[1]
    bf = lambda w: w.astype(_bf16)
    xs = x.reshape(n, d)
    k_s = v_s = None
    for layer in range(n_a + n_b):
        if layer < n_a:
            w_in = a_w_in[layer]
            proj = norm_matmul(xs, a_norm[layer], bf(w_in[:, :4 * d]), _f32)
            gb = gdn_gates(xs, a_norm[layer], w_in[:, 4 * d:], a_log[layer], a_dt_bias[layer])
            og = gdn_core(proj, gb, a_conv[layer], a_out_norm[layer], batch, seq, gdn_heads)
            xs = matmul_residual(og, bf(a_w_out[layer]), xs)
        else:
            j = layer - n_a
            if j == 0:
                k_s = norm_matmul(xs, kv_norm, bf(w_kv[:, :d]), _bf16, head_gain=k_norm)
                v_s = norm_matmul(xs, kv_norm, bf(w_kv[:, d:]), _bf16)
            q = norm_matmul(xs, b_norm[j], bf(b_w_q[j]), _bf16, head_gain=q_norm[j])
            o = sb_attention(q, k_s, v_s, batch, seq)
            xs = matmul_residual(o, bf(b_w_out[j]), xs)
        xs = conv_ffn(xs, ffn_norm[layer], bf(ffn_w_up[layer]), ffn_conv[layer], bf(ffn_w_down[layer]), seq)
    return xs.reshape(batch, seq, d)
```

```python
import functools

import jax
import jax.numpy as jnp
from jax import lax
from jax.experimental import pallas as pl
from jax.experimental.pallas import tpu as pltpu

EPS = 1e-6
HEAD_DIM = 128
GDN_CHUNK = 64
FFN_CONV_W = 3
F32_SUBLANES = 8
BF16_SUBLANES = 16
SB_KEY_BLOCK = 128
SB_STICK_EXHAUSTED = 104.0
VMEM_LIMIT_BYTES = 56 * 1024 * 1024

_f32 = jnp.float32
_bf16 = jnp.bfloat16


def _params(*semantics):
    return pltpu.CompilerParams(dimension_semantics=semantics, vmem_limit_bytes=VMEM_LIMIT_BYTES)


def _rms(x, gain):
    ms = jnp.mean(x * x, axis=-1, keepdims=True)
    return (x * lax.rsqrt(ms + EPS)) * gain


def _softplus(x):
    return jnp.maximum(x, 0.0) + jnp.log(1.0 + jnp.exp(-jnp.abs(x)))


def _sigmoid(x):
    return 1.0 / (1.0 + jnp.exp(-x))


def _dot(a, b):
    return jnp.dot(a, b, preferred_element_type=_f32)


def _dot_nt(a, b):
    return lax.dot_general(a, b, (((1,), (1,)), ((), ())), preferred_element_type=_f32)


def _dot_tn(a, b):
    return lax.dot_general(a, b, (((0,), (0,)), ((), ())), preferred_element_type=_f32)


def _split_bf16(x, pieces):
    out = []
    for _ in range(pieces):
        hi = x.astype(_bf16)
        out.append(hi)
        x = x - hi.astype(_f32)
    return out


def _norm_matmul_kernel(x_ref, g_ref, w_ref, *rest, head_norm):
    if head_norm:
        hg_ref, o_ref, xn_ref = rest
    else:
        o_ref, xn_ref = rest

    @pl.when(pl.program_id(1) == 0)
    def _():
        xn_ref[...] = _rms(x_ref[...], g_ref[...]).astype(xn_ref.dtype)

    y = _dot(xn_ref[...], w_ref[...])
    if head_norm:
        parts = []
        for h in range(y.shape[1] // HEAD_DIM):
            parts.append(_rms(y[:, h * HEAD_DIM:(h + 1) * HEAD_DIM], hg_ref[...]))
        y = jnp.concatenate(parts, axis=1)
    o_ref[...] = y.astype(o_ref.dtype)


def norm_matmul(x, gain, w, out_dtype, head_gain=None, tm=1024, tn=1024):
    n, d = x.shape
    nout = w.shape[1]
    tm, tn = min(tm, n), min(tn, nout)
    in_specs = [pl.BlockSpec((tm, d), lambda i, j: (i, 0)),
                pl.BlockSpec((1, d), lambda i, j: (0, 0)),
                pl.BlockSpec((d, tn), lambda i, j: (0, j))]
    args = [x, gain.reshape(1, d), w]
    if head_gain is not None:
        in_specs.append(pl.BlockSpec((1, HEAD_DIM), lambda i, j: (0, 0)))
        args.append(head_gain.reshape(1, HEAD_DIM))
    return pl.pallas_call(
        functools.partial(_norm_matmul_kernel, head_norm=head_gain is not None),
        grid=(n // tm, nout // tn),
        in_specs=in_specs,
        out_specs=pl.BlockSpec((tm, tn), lambda i, j: (i, j)),
        out_shape=jax.ShapeDtypeStruct((n, nout), out_dtype),
        scratch_shapes=[pltpu.VMEM((tm, d), _bf16)],
        compiler_params=_params("parallel", "arbitrary"),
        name="norm_matmul",
    )(*args)


def _matmul_residual_kernel(a_ref, w_ref, x_ref, o_ref):
    o_ref[...] = x_ref[...] + _dot(a_ref[...], w_ref[...])


def matmul_residual(a, w, x, tm=1024):
    n, k = a.shape
    d = w.shape[1]
    tm = min(tm, n)
    return pl.pallas_call(
        _matmul_residual_kernel,
        grid=(n // tm,),
        in_specs=[pl.BlockSpec((tm, k), lambda i: (i, 0)),
                  pl.BlockSpec((k, d), lambda i: (0, 0)),
                  pl.BlockSpec((tm, d), lambda i: (i, 0))],
        out_specs=pl.BlockSpec((tm, d), lambda i: (i, 0)),
        out_shape=jax.ShapeDtypeStruct((n, d), _f32),
        compiler_params=_params("parallel"),
        name="matmul_residual",
    )(a, w, x)


def _gdn_gate_kernel(x_ref, g_ref, w_ref, alog_ref, dtb_ref, o_ref, *, heads):
    tm = x_ref.shape[0]
    xn = _rms(x_ref[...], g_ref[...]).astype(_bf16)
    y = _dot(xn, w_ref[...])
    g = -jnp.exp(alog_ref[...]) * _softplus(y + dtb_ref[...])
    beta = _sigmoid(y)
    r = lax.broadcasted_iota(jnp.int32, (tm, tm), 0)
    c = lax.broadcasted_iota(jnp.int32, (tm, tm), 1)
    tri = ((r // GDN_CHUNK == c // GDN_CHUNK) & (r >= c)).astype(_bf16)
    gc = sum(_dot(tri, piece) for piece in _split_bf16(g, 3))
    lane = lax.broadcasted_iota(jnp.int32, y.shape, 1)
    o_ref[...] = jnp.where(lane < heads, gc, beta)


def gdn_gates(x, gain, w_ab, a_log, dt_bias, tm=512):
    n, d = x.shape
    heads = a_log.shape[0]
    tm = min(tm, n)
    pad = lambda v: jnp.zeros((1, HEAD_DIM), _f32).at[0, :heads].set(v.astype(_f32))
    w = jnp.zeros((d, HEAD_DIM), _bf16).at[:, :2 * heads].set(w_ab.astype(_bf16))
    return pl.pallas_call(
        functools.partial(_gdn_gate_kernel, heads=heads),
        grid=(n // tm,),
        in_specs=[pl.BlockSpec((tm, d), lambda i: (i, 0)),
                  pl.BlockSpec((1, d), lambda i: (0, 0)),
                  pl.BlockSpec((d, HEAD_DIM), lambda i: (0, 0)),
                  pl.BlockSpec((1, HEAD_DIM), lambda i: (0, 0)),
                  pl.BlockSpec((1, HEAD_DIM), lambda i: (0, 0))],
        out_specs=pl.BlockSpec((tm, HEAD_DIM), lambda i: (i, 0)),
        out_shape=jax.ShapeDtypeStruct((n, HEAD_DIM), _f32),
        compiler_params=_params("parallel"),
        name="gdn_gates",
    )(x, gain.reshape(1, d), w, pad(a_log), pad(dt_bias))


def _bdot(a, b):
    return jnp.einsum("cij,cjk->cik", a, b, preferred_element_type=_f32)


def _bdot_nt(a, b):
    return jnp.einsum("cid,cjd->cij", a, b, preferred_element_type=_f32)


def _bdot3(a, b):
    (ah, al), (bh, bl) = a, b
    return _bdot(ah, bh) + (_bdot(ah, bl) + _bdot(al, bh))


def _unit_lower_inverse(low, eye):
    x = eye - low
    p = _split_bf16(low, 2)
    for _ in range(5):
        p = _split_bf16(_bdot3(p, p), 2)
        x = x + _bdot3(_split_bf16(x, 2), p)
    return x


def _gdn_core_kernel(q_ref, k_ref, v_ref, z_ref, gb_ref, cq_ref, ck_ref, cv_ref, gain_ref, o_ref,
                     halo_ref, s_ref, *, heads, hp):
    tb = q_ref.shape[0]
    hd = HEAD_DIM
    cl = GDN_CHUNK
    nc = tb // cl
    h0 = pl.program_id(1) * hp

    @pl.when(pl.program_id(2) == 0)
    def _():
        halo_ref[...] = jnp.zeros_like(halo_ref)
        s_ref[...] = jnp.zeros_like(s_ref)

    def conv_silu(x_ref, w_ref, idx):
        raw = x_ref[...]
        ext = jnp.concatenate([halo_ref[idx], raw], axis=0)
        w = w_ref[...]
        y = w[3:4] * raw
        for d in range(1, 4):
            y = y + w[3 - d:4 - d] * ext[F32_SUBLANES - d:F32_SUBLANES - d + tb]
        halo_ref[idx] = raw[tb - F32_SUBLANES:]
        return y * _sigmoid(y)

    def l2n(x):
        return x * lax.rsqrt(jnp.sum(x * x, axis=-1, keepdims=True) + EPS)

    qf = conv_silu(q_ref, cq_ref, 0)
    kf = conv_silu(k_ref, ck_ref, 1)
    vf = conv_silu(v_ref, cv_ref, 2)
    gb = gb_ref[...]
    lane = lax.broadcasted_iota(jnp.int32, gb.shape, 1)
    ri = lax.broadcasted_iota(jnp.int32, (cl, cl), 0)
    ci = lax.broadcasted_iota(jnp.int32, (cl, cl), 1)
    eye_mask, tril, strict = ri == ci, ri >= ci, ri > ci
    eye = eye_mask.astype(_f32)

    pre = []
    for hh in range(hp):
        cols = slice(hh * hd, (hh + 1) * hd)
        chunks = lambda t: t.reshape(nc, cl, hd)
        q = chunks(l2n(qf[:, cols]) * (hd ** -0.5))
        k = chunks(l2n(kf[:, cols]))
        v = chunks(vf[:, cols])
        gcol = jnp.sum(jnp.where(lane == h0 + hh, gb, 0.0), axis=-1, keepdims=True)
        bcol = jnp.sum(jnp.where(lane == h0 + hh + heads, gb, 0.0), axis=-1, keepdims=True)
        gc = chunks(jnp.broadcast_to(gcol, (tb, hd)))
        bt = chunks(jnp.broadcast_to(bcol, (tb, hd)))
        gc_sq = gc[:, :, :cl]
        grow = jnp.sum(jnp.where(eye_mask, gc_sq, 0.0), axis=1, keepdims=True)
        decay = jnp.where(tril, jnp.exp(gc_sq - grow), 0.0)
        eg = jnp.exp(gc)
        kb = k * bt
        k16 = k.astype(_bf16)
        low = jnp.where(strict, _bdot_nt(kb.astype(_bf16), k16) * decay, 0.0)
        tinv = _unit_lower_inverse(low, eye)
        rhs = jnp.concatenate([v * bt, kb * eg], axis=2)
        sol = _bdot3(_split_bf16(tinv, 2), _split_bf16(rhs, 2))
        qk = jnp.where(tril, _bdot_nt(q.astype(_bf16), k16) * decay, 0.0)
        gl = gc[:, cl - 1:cl, :]
        pre.append(dict(u=sol[:, :, :hd], w=sol[:, :, hd:].astype(_bf16), qd=(q * eg).astype(_bf16),
                        qk=qk.astype(_bf16), kd=(k * jnp.exp(gl - gc)).astype(_bf16), egl=jnp.exp(gl)))

    state = [s_ref[hh] for hh in range(hp)]
    outs = [[] for _ in range(hp)]
    for c in range(nc):
        for hh in range(hp):
            p = pre[hh]
            s = state[hh]
            s16 = s.astype(_bf16)
            vn16 = (p["u"][c] - _dot(p["w"][c], s16)).astype(_bf16)
            outs[hh].append(_dot(p["qd"][c], s16) + _dot(p["qk"][c], vn16))
            state[hh] = s * p["egl"][c] + _dot_tn(p["kd"][c], vn16)

    for hh in range(hp):
        cols = slice(hh * hd, (hh + 1) * hd)
        s_ref[hh] = state[hh]
        z = z_ref[:, cols]
        o = jnp.concatenate(outs[hh], axis=0)
        o_ref[:, cols] = (_rms(o, gain_ref[...]) * (z * _sigmoid(z))).astype(o_ref.dtype)


def gdn_core(proj, gb, conv_w, out_gain, batch, seq, heads, tb=512, hp=2):
    n = proj.shape[0]
    tb = min(tb, seq)
    nt = seq // tb
    hg = heads // hp
    width = hp * HEAD_DIM
    row = lambda b, g, t: b * nt + t
    col = lambda off: (lambda b, g, t: (row(b, g, t), off * hg + g))
    ccol = lambda off: (lambda b, g, t: (0, off * hg + g))
    blk = (tb, width)
    return pl.pallas_call(
        functools.partial(_gdn_core_kernel, heads=heads, hp=hp),
        grid=(batch, hg, nt),
        in_specs=[pl.BlockSpec(blk, col(0)), pl.BlockSpec(blk, col(1)), pl.BlockSpec(blk, col(2)),
                  pl.BlockSpec(blk, col(3)),
                  pl.BlockSpec((tb, HEAD_DIM), lambda b, g, t: (row(b, g, t), 0)),
                  pl.BlockSpec((4, width), ccol(0)), pl.BlockSpec((4, width), ccol(1)),
                  pl.BlockSpec((4, width), ccol(2)),
                  pl.BlockSpec((1, HEAD_DIM), lambda b, g, t: (0, 0))],
        out_specs=pl.BlockSpec(blk, lambda b, g, t: (row(b, g, t), g)),
        out_shape=jax.ShapeDtypeStruct((n, heads * HEAD_DIM), _bf16),
        scratch_shapes=[pltpu.VMEM((3, F32_SUBLANES, width), _f32),
                        pltpu.VMEM((hp, HEAD_DIM, HEAD_DIM), _f32)],
        compiler_params=_params("parallel", "parallel", "arbitrary"),
        name="gdn_core",
    )(proj, proj, proj, proj, gb, conv_w, conv_w, conv_w, out_gain.reshape(1, HEAD_DIM))


def _ffn_kernel(x_ref, xh_ref, g_ref, wg_ref, wu_ref, cg_ref, cu_ref, wd_ref, o_ref, xn_ref, acc_ref,
                *, blocks_per_seq, rs):
    tm = x_ref.shape[0]
    halo = BF16_SUBLANES
    i, f = pl.program_id(0), pl.program_id(1)

    @pl.when(f == 0)
    def _():
        xn_ref[halo:, :] = _rms(x_ref[...], g_ref[...]).astype(xn_ref.dtype)
        prev = jnp.where(i % blocks_per_seq == 0, 0.0, _rms(xh_ref[...], g_ref[...]))
        xn_ref[:halo, :] = prev.astype(xn_ref.dtype)
        acc_ref[...] = jnp.zeros_like(acc_ref)

    def conv_branch(xr, w_ref, c_ref):
        u = _dot(xr, w_ref[...])
        cw = c_ref[...]
        return (cw[2:3] * u[halo:] + cw[1:2] * u[halo - 1:halo - 1 + rs]
                + cw[0:1] * u[halo - 2:halo - 2 + rs])

    def up_conv(r):
        xr = xn_ref[r * rs:r * rs + halo + rs, :]
        gate = conv_branch(xr, wg_ref, cg_ref)
        up = conv_branch(xr, wu_ref, cu_ref)
        return ((gate * _sigmoid(gate)) * up).astype(_bf16)

    acts = {}
    for r in range(tm // rs + 1):
        if r < tm // rs:
            acts[r] = up_conv(r)
        if r >= 1:
            rows = slice((r - 1) * rs, r * rs)
            acc_ref[rows, :] += _dot(acts.pop(r - 1), wd_ref[...])

    @pl.when(f == pl.num_programs(1) - 1)
    def _():
        o_ref[...] = x_ref[...] + acc_ref[...]


def conv_ffn(x, gain, w_up, conv_w, w_down, seq, tm=1024, tf=256, rs=256):
    n, d = x.shape
    ff = w_down.shape[0]
    tm, tf = min(tm, seq), min(tf, ff)
    nf = ff // tf
    hb = tm // BF16_SUBLANES
    return pl.pallas_call(
        functools.partial(_ffn_kernel, blocks_per_seq=seq // tm, rs=min(rs, tm)),
        grid=(n // tm, nf),
        in_specs=[pl.BlockSpec((tm, d), lambda i, f: (i, 0)),
                  pl.BlockSpec((BF16_SUBLANES, d), lambda i, f: (jnp.maximum(i * hb - 1, 0), 0)),
                  pl.BlockSpec((1, d), lambda i, f: (0, 0)),
                  pl.BlockSpec((d, tf), lambda i, f: (0, f)),
                  pl.BlockSpec((d, tf), lambda i, f: (0, nf + f)),
                  pl.BlockSpec((FFN_CONV_W, tf), lambda i, f: (0, f)),
                  pl.BlockSpec((FFN_CONV_W, tf), lambda i, f: (0, nf + f)),
                  pl.BlockSpec((tf, d), lambda i, f: (f, 0))],
        out_specs=pl.BlockSpec((tm, d), lambda i, f: (i, 0)),
        out_shape=jax.ShapeDtypeStruct((n, d), _f32),
        scratch_shapes=[pltpu.VMEM((BF16_SUBLANES + tm, d), _bf16), pltpu.VMEM((tm, d), _f32)],
        compiler_params=_params("parallel", "arbitrary"),
        name="conv_ffn",
    )(x, x, gain.reshape(1, d), w_up, w_up, conv_w, conv_w, w_down)


def _sb_kernel(q_ref, k_ref, v_ref, o_ref, c_ref, acc_ref, *, scale, hp):
    tq = q_ref.shape[0]
    tk = SB_KEY_BLOCK
    hd = HEAD_DIM
    nsub = tq // tk
    step = 2 if nsub % 2 == 0 else 1
    i = pl.program_id(2)
    rr = lax.broadcasted_iota(jnp.int32, (tk, tk), 0)
    cc = lax.broadcasted_iota(jnp.int32, (tk, tk), 1)
    tri = (rr >= cc).astype(_bf16)
    c_ref[...] = jnp.zeros_like(c_ref)
    acc_ref[...] = jnp.zeros_like(acc_ref)

    def process(row0, kstart, masked):
        rows = slice(row0, tq)
        for hh in range(hp):
            cols = slice(hh * hd, (hh + 1) * hd)
            kb = k_ref[pl.ds(kstart, tk), cols]
            vb = v_ref[pl.ds(kstart, tk), cols]
            z = _dot_nt(q_ref[rows, cols], kb) * scale
            sp = _softplus(z)
            if masked:
                t_loc = lax.broadcasted_iota(jnp.int32, z.shape, 0)
                s_loc = lax.broadcasted_iota(jnp.int32, z.shape, 1)
                mask = s_loc < t_loc
                sp = jnp.where(mask, sp, 0.0)
            cum = sum(_dot(piece, tri) for piece in _split_bf16(sp, 2))
            c = c_ref[rows, cols]
            a = jnp.exp(z - cum - c)
            if masked:
                a = jnp.where(mask, a, 0.0)
            acc_ref[rows, cols] += _dot(a.astype(_bf16), vb)
            c_ref[rows, cols] = c + cum[:, 0:1]

    for sub in reversed(range(nsub)):
        process(sub * tk, pl.multiple_of(i * tq + sub * tk, tk), True)

    def cond(carry):
        j, go = carry
        return jnp.logical_and(j >= 0, go > 0)

    def body(carry):
        j, _ = carry
        for b in range(step):
            process(0, pl.multiple_of((j - b) * tk, tk), False)
        go = (jnp.min(c_ref[...]) <= SB_STICK_EXHAUSTED).astype(jnp.int32)
        return j - step, go

    lax.while_loop(cond, body, (i * nsub - 1, jnp.int32(1)))
    o_ref[...] = acc_ref[...].astype(o_ref.dtype)


def sb_attention(q, k, v, batch, seq, tq=512, hp=2):
    n, d = q.shape
    heads = d // HEAD_DIM
    tq = min(tq, seq)
    nq = seq // tq
    width = hp * HEAD_DIM
    return pl.pallas_call(
        functools.partial(_sb_kernel, scale=HEAD_DIM ** -0.5, hp=hp),
        grid=(batch, heads // hp, nq),
        in_specs=[pl.BlockSpec((tq, width), lambda b, g, i: (b * nq + i, g)),
                  pl.BlockSpec((seq, width), lambda b, g, i: (b, g)),
                  pl.BlockSpec((seq, width), lambda b, g, i: (b, g))],
        out_specs=pl.BlockSpec((tq, width), lambda b, g, i: (b * nq + i, g)),
        out_shape=jax.ShapeDtypeStruct((n, d), _bf16),
        scratch_shapes=[pltpu.VMEM((tq, width), _f32), pltpu.VMEM((tq, width), _f32)],
        compiler_params=_params("parallel", "parallel", "arbitrary"),
        name="sb_attention",
    )(q, k, v)


def kernel(x, a_norm, a_w_in, a_conv, a_log, a_dt_bias, a_out_norm, a_w_out, kv_norm, w_kv, k_norm, b_norm, b_w_q, q_norm, b_w_out, ffn_norm, ffn_w_up, ffn_conv, ffn_w_down):
    batch, seq, d = x.shape
    n = batch * seq
    n_a = a_w_in.shape[0]
    n_b = b_w_q.shape[0]
    gdn_heads = a_log.shape[1]
    bf = lambda w: w.astype(_bf16)
    xs = x.reshape(n, d)
    k_s = v_s = None
    for layer in range(n_a + n_b):
        if layer < n_a:
            w_in = a_w_in[layer]
            proj = norm_matmul(xs, a_norm[layer], bf(w_in[:, :4 * d]), _f32)
            gb = gdn_gates(xs, a_norm[layer], w_in[:, 4 * d:], a_log[layer], a_dt_bias[layer])
            og = gdn_core(proj, gb, a_conv[layer], a_out_norm[layer], batch, seq, gdn_heads)
            xs = matmul_residual(og, bf(a_w_out[layer]), xs)
        else:
            j = layer - n_a
            if j == 0:
                k_s = norm_matmul(xs, kv_norm, bf(w_kv[:, :d]), _bf16, head_gain=k_norm)
                v_s = norm_matmul(xs, kv_norm, bf(w_kv[:, d:]), _bf16)
            q = norm_matmul(xs, b_norm[j], bf(b_w_q[j]), _bf16, head_gain=q_norm[j])
            o = sb_attention(q, k_s, v_s, batch, seq)
            xs = matmul_residual(o, bf(b_w_out[j]), xs)
        xs = conv_ffn(xs, ffn_norm[layer], bf(ffn_w_up[layer]), ffn_conv[layer], bf(ffn_w_down[layer]), seq)
    return xs.reshape(batch, seq, d)
```

```python
import functools

import jax
import jax.numpy as jnp
from jax import lax
from jax.experimental import pallas as pl
from jax.experimental.pallas import tpu as pltpu

EPS = 1e-6
HEAD_DIM = 128
GDN_CHUNK = 64
FFN_CONV_W = 3
F32_SUBLANES = 8
BF16_SUBLANES = 16
SB_KEY_BLOCK = 256
SB_STICK_EXHAUSTED = 104.0
VMEM_LIMIT_BYTES = 56 * 1024 * 1024

_f32 = jnp.float32
_bf16 = jnp.bfloat16


def _params(*semantics):
    return pltpu.CompilerParams(dimension_semantics=semantics, vmem_limit_bytes=VMEM_LIMIT_BYTES)


def _rms(x, gain):
    ms = jnp.mean(x * x, axis=-1, keepdims=True)
    return (x * lax.rsqrt(ms + EPS)) * gain


def _softplus(x):
    return jnp.maximum(x, 0.0) + jnp.log(1.0 + jnp.exp(-jnp.abs(x)))


def _sigmoid(x):
    return 1.0 / (1.0 + jnp.exp(-x))


def _dot(a, b):
    return jnp.dot(a, b, preferred_element_type=_f32)


def _dot_nt(a, b):
    return lax.dot_general(a, b, (((1,), (1,)), ((), ())), preferred_element_type=_f32)


def _dot_tn(a, b):
    return lax.dot_general(a, b, (((0,), (0,)), ((), ())), preferred_element_type=_f32)


def _split_bf16(x, pieces):
    out = []
    for _ in range(pieces):
        hi = x.astype(_bf16)
        out.append(hi)
        x = x - hi.astype(_f32)
    return out


def _norm_matmul_kernel(x_ref, g_ref, w_ref, *rest, head_norm):
    if head_norm:
        hg_ref, o_ref, xn_ref = rest
    else:
        o_ref, xn_ref = rest

    @pl.when(pl.program_id(1) == 0)
    def _():
        xn_ref[...] = _rms(x_ref[...], g_ref[...]).astype(xn_ref.dtype)

    y = _dot(xn_ref[...], w_ref[...])
    if head_norm:
        parts = []
        for h in range(y.shape[1] // HEAD_DIM):
            parts.append(_rms(y[:, h * HEAD_DIM:(h + 1) * HEAD_DIM], hg_ref[...]))
        y = jnp.concatenate(parts, axis=1)
    o_ref[...] = y.astype(o_ref.dtype)


def norm_matmul(x, gain, w, out_dtype, head_gain=None, tm=1024, tn=1024):
    n, d = x.shape
    nout = w.shape[1]
    tm, tn = min(tm, n), min(tn, nout)
    in_specs = [pl.BlockSpec((tm, d), lambda i, j: (i, 0)),
                pl.BlockSpec((1, d), lambda i, j: (0, 0)),
                pl.BlockSpec((d, tn), lambda i, j: (0, j))]
    args = [x, gain.reshape(1, d), w]
    if head_gain is not None:
        in_specs.append(pl.BlockSpec((1, HEAD_DIM), lambda i, j: (0, 0)))
        args.append(head_gain.reshape(1, HEAD_DIM))
    return pl.pallas_call(
        functools.partial(_norm_matmul_kernel, head_norm=head_gain is not None),
        grid=(n // tm, nout // tn),
        in_specs=in_specs,
        out_specs=pl.BlockSpec((tm, tn), lambda i, j: (i, j)),
        out_shape=jax.ShapeDtypeStruct((n, nout), out_dtype),
        scratch_shapes=[pltpu.VMEM((tm, d), _bf16)],
        compiler_params=_params("parallel", "arbitrary"),
        name="norm_matmul",
    )(*args)


def _matmul_residual_kernel(a_ref, w_ref, x_ref, o_ref):
    o_ref[...] = x_ref[...] + _dot(a_ref[...], w_ref[...])


def matmul_residual(a, w, x, tm=1024):
    n, k = a.shape
    d = w.shape[1]
    tm = min(tm, n)
    return pl.pallas_call(
        _matmul_residual_kernel,
        grid=(n // tm,),
        in_specs=[pl.BlockSpec((tm, k), lambda i: (i, 0)),
                  pl.BlockSpec((k, d), lambda i: (0, 0)),
                  pl.BlockSpec((tm, d), lambda i: (i, 0))],
        out_specs=pl.BlockSpec((tm, d), lambda i: (i, 0)),
        out_shape=jax.ShapeDtypeStruct((n, d), _f32),
        compiler_params=_params("parallel"),
        name="matmul_residual",
    )(a, w, x)


def _gdn_gate_kernel(x_ref, g_ref, w_ref, alog_ref, dtb_ref, o_ref, *, heads):
    tm = x_ref.shape[0]
    xn = _rms(x_ref[...], g_ref[...]).astype(_bf16)
    y = _dot(xn, w_ref[...])
    g = -jnp.exp(alog_ref[...]) * _softplus(y + dtb_ref[...])
    beta = _sigmoid(y)
    r = lax.broadcasted_iota(jnp.int32, (tm, tm), 0)
    c = lax.broadcasted_iota(jnp.int32, (tm, tm), 1)
    tri = ((r // GDN_CHUNK == c // GDN_CHUNK) & (r >= c)).astype(_bf16)
    gc = sum(_dot(tri, piece) for piece in _split_bf16(g, 3))
    lane = lax.broadcasted_iota(jnp.int32, y.shape, 1)
    o_ref[...] = jnp.where(lane < heads, gc, beta)


def gdn_gates(x, gain, w_ab, a_log, dt_bias, tm=512):
    n, d = x.shape
    heads = a_log.shape[0]
    tm = min(tm, n)
    pad = lambda v: jnp.zeros((1, HEAD_DIM), _f32).at[0, :heads].set(v.astype(_f32))
    w = jnp.zeros((d, HEAD_DIM), _bf16).at[:, :2 * heads].set(w_ab.astype(_bf16))
    return pl.pallas_call(
        functools.partial(_gdn_gate_kernel, heads=heads),
        grid=(n // tm,),
        in_specs=[pl.BlockSpec((tm, d), lambda i: (i, 0)),
                  pl.BlockSpec((1, d), lambda i: (0, 0)),
                  pl.BlockSpec((d, HEAD_DIM), lambda i: (0, 0)),
                  pl.BlockSpec((1, HEAD_DIM), lambda i: (0, 0)),
                  pl.BlockSpec((1, HEAD_DIM), lambda i: (0, 0))],
        out_specs=pl.BlockSpec((tm, HEAD_DIM), lambda i: (i, 0)),
        out_shape=jax.ShapeDtypeStruct((n, HEAD_DIM), _f32),
        compiler_params=_params("parallel"),
        name="gdn_gates",
    )(x, gain.reshape(1, d), w, pad(a_log), pad(dt_bias))


def _bdot(a, b):
    return jnp.einsum("cij,cjk->cik", a, b, preferred_element_type=_f32)


def _bdot_nt(a, b):
    return jnp.einsum("cid,cjd->cij", a, b, preferred_element_type=_f32)


def _bdot3_stacked(lhs_list, b):
    n = lhs_list[0].shape[1]
    bh, bl = _split_bf16(b, 2)
    his, los = zip(*(_split_bf16(a, 2) for a in lhs_list))
    m = len(lhs_list)
    big = _bdot(jnp.concatenate(his + los, axis=1), bh)
    sml = _bdot(jnp.concatenate(his, axis=1), bl)
    rows = lambda t, i: t[:, i * n:(i + 1) * n]
    return [rows(big, i) + (rows(big, m + i) + rows(sml, i)) for i in range(m)]


def _unit_lower_inverse(low, eye):
    x = eye - low
    (p,) = _bdot3_stacked([low], low)
    for _ in range(4):
        xp, p = _bdot3_stacked([x, p], p)
        x = x + xp
    (xp,) = _bdot3_stacked([x], p)
    return x + xp


def _gdn_core_kernel(q_ref, k_ref, v_ref, z_ref, gb_ref, cq_ref, ck_ref, cv_ref, gain_ref, o_ref,
                     halo_ref, s_ref, *, heads, hp):
    tb = q_ref.shape[0]
    hd = HEAD_DIM
    cl = GDN_CHUNK
    nc = tb // cl
    h0 = pl.program_id(1) * hp

    @pl.when(pl.program_id(2) == 0)
    def _():
        halo_ref[...] = jnp.zeros_like(halo_ref)
        s_ref[...] = jnp.zeros_like(s_ref)

    def conv_silu(x_ref, w_ref, idx):
        raw = x_ref[...]
        ext = jnp.concatenate([halo_ref[idx], raw], axis=0)
        w = w_ref[...]
        y = w[3:4] * raw
        for d in range(1, 4):
            y = y + w[3 - d:4 - d] * ext[F32_SUBLANES - d:F32_SUBLANES - d + tb]
        halo_ref[idx] = raw[tb - F32_SUBLANES:]
        return y * _sigmoid(y)

    def l2n(x):
        return x * lax.rsqrt(jnp.sum(x * x, axis=-1, keepdims=True) + EPS)

    qf = conv_silu(q_ref, cq_ref, 0)
    kf = conv_silu(k_ref, ck_ref, 1)
    vf = conv_silu(v_ref, cv_ref, 2)
    gb = gb_ref[...]
    lane = lax.broadcasted_iota(jnp.int32, gb.shape, 1)
    ri = lax.broadcasted_iota(jnp.int32, (cl, cl), 0)
    ci = lax.broadcasted_iota(jnp.int32, (cl, cl), 1)
    eye_mask, tril, strict = ri == ci, ri >= ci, ri > ci
    eye = eye_mask.astype(_f32)

    def per_head(fn):
        return jnp.concatenate([fn(hh, slice(hh * hd, (hh + 1) * hd)).reshape(nc, cl, hd)
                                for hh in range(hp)], axis=0)

    def gate_col(lane_of_head):
        return lambda hh, cols: jnp.broadcast_to(
            jnp.sum(jnp.where(lane == lane_of_head + hh, gb, 0.0), axis=-1, keepdims=True), (tb, hd))

    q = per_head(lambda hh, cols: l2n(qf[:, cols]) * (hd ** -0.5))
    k = per_head(lambda hh, cols: l2n(kf[:, cols]))
    v = per_head(lambda hh, cols: vf[:, cols])
    gc = per_head(gate_col(h0))
    bt = per_head(gate_col(h0 + heads))
    gc_sq = gc[:, :, :cl]
    grow = jnp.sum(jnp.where(eye_mask, gc_sq, 0.0), axis=1, keepdims=True)
    decay = jnp.where(tril, jnp.exp(gc_sq - grow), 0.0)
    eg = jnp.exp(gc)
    kb = k * bt
    k16 = k.astype(_bf16)
    kq = _bdot_nt(jnp.concatenate([kb.astype(_bf16), q.astype(_bf16)], axis=1), k16)
    low = jnp.where(strict, kq[:, :cl] * decay, 0.0)
    qk = jnp.where(tril, kq[:, cl:] * decay, 0.0).astype(_bf16)
    tinv = _unit_lower_inverse(low, eye)
    rhs = jnp.concatenate([v * bt, kb * eg], axis=2)
    (sol,) = _bdot3_stacked([tinv], rhs)
    u = sol[:, :, :hd]
    gl = gc[:, cl - 1:cl, :]
    egl = jnp.exp(gl)
    wq = jnp.concatenate([sol[:, :, hd:].astype(_bf16), (q * eg).astype(_bf16)], axis=1)
    kd_t = jnp.swapaxes(k * jnp.exp(gl - gc), 1, 2).astype(_bf16)
    qkd = jnp.concatenate([qk, kd_t], axis=1)

    state = [s_ref[hh] for hh in range(hp)]
    outs = [[] for _ in range(hp)]
    for c in range(nc):
        for hh in range(hp):
            b = hh * nc + c
            s = state[hh]
            ws_qs = _dot(wq[b], s.astype(_bf16))
            vn16 = (u[b] - ws_qs[:cl]).astype(_bf16)
            r = _dot(qkd[b], vn16)
            outs[hh].append(ws_qs[cl:] + r[:cl])
            state[hh] = s * egl[b] + r[cl:]

    for hh in range(hp):
        cols = slice(hh * hd, (hh + 1) * hd)
        s_ref[hh] = state[hh]
        z = z_ref[:, cols]
        o = jnp.concatenate(outs[hh], axis=0)
        o_ref[:, cols] = (_rms(o, gain_ref[...]) * (z * _sigmoid(z))).astype(o_ref.dtype)


def gdn_core(proj, gb, conv_w, out_gain, batch, seq, heads, tb=256, hp=4):
    n = proj.shape[0]
    tb, hp = min(tb, seq), min(hp, heads)
    nt = seq // tb
    hg = heads // hp
    width = hp * HEAD_DIM
    row = lambda b, g, t: b * nt + t
    col = lambda off: (lambda b, g, t: (row(b, g, t), off * hg + g))
    ccol = lambda off: (lambda b, g, t: (0, off * hg + g))
    blk = (tb, width)
    return pl.pallas_call(
        functools.partial(_gdn_core_kernel, heads=heads, hp=hp),
        grid=(batch, hg, nt),
        in_specs=[pl.BlockSpec(blk, col(0)), pl.BlockSpec(blk, col(1)), pl.BlockSpec(blk, col(2)),
                  pl.BlockSpec(blk, col(3)),
                  pl.BlockSpec((tb, HEAD_DIM), lambda b, g, t: (row(b, g, t), 0)),
                  pl.BlockSpec((4, width), ccol(0)), pl.BlockSpec((4, width), ccol(1)),
                  pl.BlockSpec((4, width), ccol(2)),
                  pl.BlockSpec((1, HEAD_DIM), lambda b, g, t: (0, 0))],
        out_specs=pl.BlockSpec(blk, lambda b, g, t: (row(b, g, t), g)),
        out_shape=jax.ShapeDtypeStruct((n, heads * HEAD_DIM), _bf16),
        scratch_shapes=[pltpu.VMEM((3, F32_SUBLANES, width), _f32),
                        pltpu.VMEM((hp, HEAD_DIM, HEAD_DIM), _f32)],
        compiler_params=_params("parallel", "parallel", "arbitrary"),
        name="gdn_core",
    )(proj, proj, proj, proj, gb, conv_w, conv_w, conv_w, out_gain.reshape(1, HEAD_DIM))


def _ffn_kernel(x_ref, xh_ref, g_ref, wg_ref, wu_ref, cg_ref, cu_ref, wd_ref, o_ref, xn_ref,
                *, blocks_per_seq):
    tm = x_ref.shape[0]
    halo = BF16_SUBLANES
    i, f = pl.program_id(0), pl.program_id(1)

    @pl.when(f == 0)
    def _():
        xn_ref[halo:, :] = _rms(x_ref[...], g_ref[...]).astype(xn_ref.dtype)
        prev = jnp.where(i % blocks_per_seq == 0, 0.0, _rms(xh_ref[...], g_ref[...]))
        xn_ref[:halo, :] = prev.astype(xn_ref.dtype)
        o_ref[...] = x_ref[...]

    xn = xn_ref[...]

    def conv_branch(w_ref, c_ref):
        u = _dot(xn, w_ref[...])
        cw = c_ref[...]
        return (cw[2:3] * u[halo:] + cw[1:2] * u[halo - 1:halo - 1 + tm]
                + cw[0:1] * u[halo - 2:halo - 2 + tm])

    gate = conv_branch(wg_ref, cg_ref)
    up = conv_branch(wu_ref, cu_ref)
    act = ((gate * _sigmoid(gate)) * up).astype(_bf16)
    o_ref[...] += _dot(act, wd_ref[...])


def conv_ffn(x, gain, w_up, conv_w, w_down, seq, tm=1024, tf=256):
    n, d = x.shape
    ff = w_down.shape[0]
    tm, tf = min(tm, seq), min(tf, ff)
    assert seq % tm == 0 and ff % tf == 0 and tm % BF16_SUBLANES == 0, (seq, tm, ff, tf)
    nf = ff // tf
    hb = tm // BF16_SUBLANES
    return pl.pallas_call(
        functools.partial(_ffn_kernel, blocks_per_seq=seq // tm),
        grid=(n // tm, nf),
        in_specs=[pl.BlockSpec((tm, d), lambda i, f: (i, 0)),
                  pl.BlockSpec((BF16_SUBLANES, d), lambda i, f: (jnp.maximum(i * hb - 1, 0), 0)),
                  pl.BlockSpec((1, d), lambda i, f: (0, 0)),
                  pl.BlockSpec((d, tf), lambda i, f: (0, f)),
                  pl.BlockSpec((d, tf), lambda i, f: (0, nf + f)),
                  pl.BlockSpec((FFN_CONV_W, tf), lambda i, f: (0, f)),
                  pl.BlockSpec((FFN_CONV_W, tf), lambda i, f: (0, nf + f)),
                  pl.BlockSpec((tf, d), lambda i, f: (f, 0))],
        out_specs=pl.BlockSpec((tm, d), lambda i, f: (i, 0)),
        out_shape=jax.ShapeDtypeStruct((n, d), _f32),
        scratch_shapes=[pltpu.VMEM((BF16_SUBLANES + tm, d), _bf16)],
        compiler_params=_params("parallel", "arbitrary"),
        name="conv_ffn",
    )(x, x, gain.reshape(1, d), w_up, w_up, conv_w, conv_w, w_down)


def _sb_kernel(q_ref, k_ref, v_ref, o_ref, c_ref, acc_ref, *, scale, hp):
    tq = q_ref.shape[0]
    tk = SB_KEY_BLOCK
    hd = HEAD_DIM
    nsub = tq // tk
    i = pl.program_id(2)
    rr = lax.broadcasted_iota(jnp.int32, (tk, tk), 0)
    cc = lax.broadcasted_iota(jnp.int32, (tk, tk), 1)
    tri = (rr >= cc).astype(_bf16)
    c_ref[...] = jnp.zeros_like(c_ref)
    acc_ref[...] = jnp.zeros_like(acc_ref)

    def scores(row0, row1, kstart, masked, hh):
        cols = slice(hh * hd, (hh + 1) * hd)
        z = _dot_nt(q_ref[row0:row1, cols], k_ref[pl.ds(kstart, tk), cols]) * scale
        sp = _softplus(z)
        mask = None
        if masked:
            t_loc = lax.broadcasted_iota(jnp.int32, z.shape, 0)
            s_loc = lax.broadcasted_iota(jnp.int32, z.shape, 1)
            mask = s_loc < t_loc
            sp = jnp.where(mask, sp, 0.0)
        both = _dot(jnp.concatenate(_split_bf16(sp, 2), axis=0), tri)
        return z - (both[:row1 - row0] + both[row1 - row0:]), both[:row1 - row0, 0:1] + both[row1 - row0:, 0:1], mask

    def absorb(row0, row1, kstart, hh, z_minus_cum, total, mask):
        cols = slice(hh * hd, (hh + 1) * hd)
        c = c_ref[row0:row1, cols]
        a = jnp.exp(z_minus_cum - jnp.concatenate([c] * (tk // hd), axis=1))
        if mask is not None:
            a = jnp.where(mask, a, 0.0)
        acc_ref[row0:row1, cols] += _dot(a.astype(_bf16), v_ref[pl.ds(kstart, tk), cols])
        c_ref[row0:row1, cols] = c + total

    def process(tiles):
        pre = [[scores(*t, hh) for hh in range(hp)] for t in tiles]
        for t, per_head in zip(tiles, pre):
            for hh in range(hp):
                absorb(t[0], t[1], t[2], hh, *per_head[hh])

    process([(sub * tk, tq, pl.multiple_of(i * tq + sub * tk, tk), True) for sub in reversed(range(nsub))])

    def walk(row0, row1):
        def alive():
            return (jnp.min(c_ref[row0:row1, :]) <= SB_STICK_EXHAUSTED).astype(jnp.int32)

        def cond(carry):
            j, go = carry
            return jnp.logical_and(j >= 0, go > 0)

        def body(carry):
            j, _ = carry
            process([(row0, row1, pl.multiple_of(j * tk, tk), False)])
            return j - 1, alive()

        lax.while_loop(cond, body, (i * nsub - 1, alive()))

    groups = 2 if tq % (2 * BF16_SUBLANES) == 0 else 1
    for grp in range(groups):
        walk(grp * (tq // groups), (grp + 1) * (tq // groups))
    o_ref[...] = acc_ref[...].astype(o_ref.dtype)


def sb_attention(q, k, v, batch, seq, tq=512, hp=4):
    n, d = q.shape
    heads = d // HEAD_DIM
    tq, hp = min(tq, seq), min(hp, heads)
    nq = seq // tq
    width = hp * HEAD_DIM
    return pl.pallas_call(
        functools.partial(_sb_kernel, scale=HEAD_DIM ** -0.5, hp=hp),
        grid=(batch, heads // hp, nq),
        in_specs=[pl.BlockSpec((tq, width), lambda b, g, i: (b * nq + i, g)),
                  pl.BlockSpec((seq, width), lambda b, g, i: (b, g), pipeline_mode=pl.Buffered(1)),
                  pl.BlockSpec((seq, width), lambda b, g, i: (b, g), pipeline_mode=pl.Buffered(1))],
        out_specs=pl.BlockSpec((tq, width), lambda b, g, i: (b * nq + i, g)),
        out_shape=jax.ShapeDtypeStruct((n, d), _bf16),
        scratch_shapes=[pltpu.VMEM((tq, width), _f32), pltpu.VMEM((tq, width), _f32)],
        compiler_params=_params("parallel", "parallel", "arbitrary"),
        name="sb_attention",
    )(q, k, v)


def kernel(x, a_norm, a_w_in, a_conv, a_log, a_dt_bias, a_out_norm, a_w_out, kv_norm, w_kv, k_norm, b_norm, b_w_q, q_norm, b_w_out, ffn_norm, ffn_w_up, ffn_conv, ffn_w_down):
    batch, seq, d = x.shape
    n = batch * seq
    n_a = a_w_in.shape[0]
    n_b = b_w_q.shape[0]
    gdn_heads = a_log.shape[1]
    bf = lambda w: w.astype(_bf16)
    xs = x.reshape(n, d)
    k_s = v_s = None
    for layer in range(n_a + n_b):
        if layer < n_a:
            w_in = a_w_in[layer]
            proj = norm_matmul(xs, a_norm[layer], bf(w_in[:, :4 * d]), _f32)
            gb = gdn_gates(xs, a_norm[layer], w_in[:, 4 * d:], a_log[layer], a_dt_bias[layer])
            og = gdn_core(proj, gb, a_conv[layer], a_out_norm[layer], batch, seq, gdn_heads)
            xs = matmul_residual(og, bf(a_w_out[layer]), xs)
        else:
            j = layer - n_a
            if j == 0:
                k_s = norm_matmul(xs, kv_norm, bf(w_kv[:, :d]), _bf16, head_gain=k_norm)
                v_s = norm_matmul(xs, kv_norm, bf(w_kv[:, d:]), _bf16)
            q = norm_matmul(xs, b_norm[j], bf(b_w_q[j]), _bf16, head_gain=q_norm[j])
            o = sb_attention(q, k_s, v_s, batch, seq)
            xs = matmul_residual(o, bf(b_w_out[j]), xs)
        xs = conv_ffn(xs, ffn_norm[layer], bf(ffn_w_up[layer]), ffn_conv[layer], bf(ffn_w_down[layer]), seq)
    return xs.reshape(batch, seq, d)
```

```python
import functools

import jax
import jax.numpy as jnp
from jax import lax
from jax.experimental import pallas as pl
from jax.experimental.pallas import tpu as pltpu

EPS = 1e-6
HEAD_DIM = 128
GDN_CHUNK = 64
FFN_CONV_W = 3
F32_SUBLANES = 8
BF16_SUBLANES = 16
SB_KEY_BLOCK = 256
SB_STICK_EXHAUSTED = 104.0
VMEM_LIMIT_BYTES = 56 * 1024 * 1024

_f32 = jnp.float32
_bf16 = jnp.bfloat16


def _params(*semantics):
    return pltpu.CompilerParams(dimension_semantics=semantics, vmem_limit_bytes=VMEM_LIMIT_BYTES)


def _rms(x, gain):
    ms = jnp.mean(x * x, axis=-1, keepdims=True)
    return (x * lax.rsqrt(ms + EPS)) * gain


def _softplus(x):
    return jnp.maximum(x, 0.0) + jnp.log(1.0 + jnp.exp(-jnp.abs(x)))


def _sigmoid(x):
    return 1.0 / (1.0 + jnp.exp(-x))


def _dot(a, b):
    return jnp.dot(a, b, preferred_element_type=_f32)


def _dot_nt(a, b):
    return lax.dot_general(a, b, (((1,), (1,)), ((), ())), preferred_element_type=_f32)


def _dot_tn(a, b):
    return lax.dot_general(a, b, (((0,), (0,)), ((), ())), preferred_element_type=_f32)


def _split_bf16(x, pieces):
    out = []
    for _ in range(pieces):
        hi = x.astype(_bf16)
        out.append(hi)
        x = x - hi.astype(_f32)
    return out


def _norm_matmul_kernel(x_ref, g_ref, w_ref, *rest, head_norm, normed_tiles):
    if head_norm:
        hg_ref, o_ref, xn_ref = rest
    else:
        o_ref, xn_ref = rest

    @pl.when(pl.program_id(1) == 0)
    def _():
        xn_ref[...] = _rms(x_ref[...], g_ref[...]).astype(xn_ref.dtype)

    y = _dot(xn_ref[...], w_ref[...])
    if head_norm:
        parts = []
        for h in range(y.shape[1] // HEAD_DIM):
            parts.append(_rms(y[:, h * HEAD_DIM:(h + 1) * HEAD_DIM], hg_ref[...]))
        normed = jnp.concatenate(parts, axis=1)
        y = normed if normed_tiles is None else jnp.where(pl.program_id(1) < normed_tiles, normed, y)
    o_ref[...] = y.astype(o_ref.dtype)


def norm_matmul(x, gain, w, out_dtype, head_gain=None, normed_cols=None, tm=1024, tn=1024):
    n, d = x.shape
    nout = w.shape[1]
    tm, tn = min(tm, n), min(tn, nout)
    normed_tiles = None
    if normed_cols is not None:
        tn = min(tn, normed_cols)
        assert normed_cols % tn == 0, (normed_cols, tn)
        normed_tiles = normed_cols // tn
    in_specs = [pl.BlockSpec((tm, d), lambda i, j: (i, 0)),
                pl.BlockSpec((1, d), lambda i, j: (0, 0)),
                pl.BlockSpec((d, tn), lambda i, j: (0, j))]
    args = [x, gain.reshape(1, d), w]
    if head_gain is not None:
        in_specs.append(pl.BlockSpec((1, HEAD_DIM), lambda i, j: (0, 0)))
        args.append(head_gain.reshape(1, HEAD_DIM))
    return pl.pallas_call(
        functools.partial(_norm_matmul_kernel, head_norm=head_gain is not None, normed_tiles=normed_tiles),
        grid=(n // tm, nout // tn),
        in_specs=in_specs,
        out_specs=pl.BlockSpec((tm, tn), lambda i, j: (i, j)),
        out_shape=jax.ShapeDtypeStruct((n, nout), out_dtype),
        scratch_shapes=[pltpu.VMEM((tm, d), _bf16)],
        compiler_params=_params("parallel", "arbitrary"),
        name="norm_matmul",
    )(*args)


def _matmul_residual_kernel(a_ref, w_ref, x_ref, o_ref):
    o_ref[...] = x_ref[...] + _dot(a_ref[...], w_ref[...])


def matmul_residual(a, w, x, tm=1024):
    n, k = a.shape
    d = w.shape[1]
    tm = min(tm, n)
    return pl.pallas_call(
        _matmul_residual_kernel,
        grid=(n // tm,),
        in_specs=[pl.BlockSpec((tm, k), lambda i: (i, 0)),
                  pl.BlockSpec((k, d), lambda i: (0, 0)),
                  pl.BlockSpec((tm, d), lambda i: (i, 0))],
        out_specs=pl.BlockSpec((tm, d), lambda i: (i, 0)),
        out_shape=jax.ShapeDtypeStruct((n, d), _f32),
        compiler_params=_params("parallel"),
        name="matmul_residual",
    )(a, w, x)


def _gdn_gate_kernel(x_ref, g_ref, w_ref, alog_ref, dtb_ref, o_ref, *, heads):
    tm = x_ref.shape[0]
    xn = _rms(x_ref[...], g_ref[...]).astype(_bf16)
    y = _dot(xn, w_ref[...])
    g = -jnp.exp(alog_ref[...]) * _softplus(y + dtb_ref[...])
    beta = _sigmoid(y)
    r = lax.broadcasted_iota(jnp.int32, (tm, tm), 0)
    c = lax.broadcasted_iota(jnp.int32, (tm, tm), 1)
    tri = ((r // GDN_CHUNK == c // GDN_CHUNK) & (r >= c)).astype(_bf16)
    gc = sum(_dot(tri, piece) for piece in _split_bf16(g, 3))
    lane = lax.broadcasted_iota(jnp.int32, y.shape, 1)
    o_ref[...] = jnp.where(lane < heads, gc, beta)


def gdn_gates(x, gain, w_ab, a_log, dt_bias, tm=512):
    n, d = x.shape
    heads = a_log.shape[0]
    tm = min(tm, n)
    pad = lambda v: jnp.zeros((1, HEAD_DIM), _f32).at[0, :heads].set(v.astype(_f32))
    w = jnp.zeros((d, HEAD_DIM), _bf16).at[:, :2 * heads].set(w_ab.astype(_bf16))
    return pl.pallas_call(
        functools.partial(_gdn_gate_kernel, heads=heads),
        grid=(n // tm,),
        in_specs=[pl.BlockSpec((tm, d), lambda i: (i, 0)),
                  pl.BlockSpec((1, d), lambda i: (0, 0)),
                  pl.BlockSpec((d, HEAD_DIM), lambda i: (0, 0)),
                  pl.BlockSpec((1, HEAD_DIM), lambda i: (0, 0)),
                  pl.BlockSpec((1, HEAD_DIM), lambda i: (0, 0))],
        out_specs=pl.BlockSpec((tm, HEAD_DIM), lambda i: (i, 0)),
        out_shape=jax.ShapeDtypeStruct((n, HEAD_DIM), _f32),
        compiler_params=_params("parallel"),
        name="gdn_gates",
    )(x, gain.reshape(1, d), w, pad(a_log), pad(dt_bias))


def _bdot(a, b):
    return jnp.einsum("cij,cjk->cik", a, b, preferred_element_type=_f32)


def _bdot_nt(a, b):
    return jnp.einsum("cid,cjd->cij", a, b, preferred_element_type=_f32)


def _bdot3_stacked(lhs_list, b):
    n = lhs_list[0].shape[1]
    bh, bl = _split_bf16(b, 2)
    his, los = zip(*(_split_bf16(a, 2) for a in lhs_list))
    m = len(lhs_list)
    big = _bdot(jnp.concatenate(his + los, axis=1), bh)
    sml = _bdot(jnp.concatenate(his, axis=1), bl)
    rows = lambda t, i: t[:, i * n:(i + 1) * n]
    return [rows(big, i) + (rows(big, m + i) + rows(sml, i)) for i in range(m)]


def _unit_lower_inverse(low, eye):
    x = eye - low
    (p,) = _bdot3_stacked([low], low)
    for _ in range(4):
        xp, p = _bdot3_stacked([x, p], p)
        x = x + xp
    (xp,) = _bdot3_stacked([x], p)
    return x + xp


def _gdn_core_kernel(q_ref, k_ref, v_ref, z_ref, gb_ref, cq_ref, ck_ref, cv_ref, gain_ref, o_ref,
                     halo_ref, s_ref, *, heads, hp):
    tb = q_ref.shape[0]
    hd = HEAD_DIM
    cl = GDN_CHUNK
    nc = tb // cl
    h0 = pl.program_id(1) * hp

    @pl.when(pl.program_id(2) == 0)
    def _():
        halo_ref[...] = jnp.zeros_like(halo_ref)
        s_ref[...] = jnp.zeros_like(s_ref)

    def conv_silu(x_ref, w_ref, idx):
        raw = x_ref[...].astype(_f32)
        ext = jnp.concatenate([halo_ref[idx], raw], axis=0)
        w = w_ref[...]
        y = w[3:4] * raw
        for d in range(1, 4):
            y = y + w[3 - d:4 - d] * ext[F32_SUBLANES - d:F32_SUBLANES - d + tb]
        halo_ref[idx] = raw[tb - F32_SUBLANES:]
        return y * _sigmoid(y)

    def l2n(x):
        return x * lax.rsqrt(jnp.sum(x * x, axis=-1, keepdims=True) + EPS)

    qf = conv_silu(q_ref, cq_ref, 0)
    kf = conv_silu(k_ref, ck_ref, 1)
    vf = conv_silu(v_ref, cv_ref, 2)
    gb = gb_ref[...]
    lane = lax.broadcasted_iota(jnp.int32, gb.shape, 1)
    ri = lax.broadcasted_iota(jnp.int32, (cl, cl), 0)
    ci = lax.broadcasted_iota(jnp.int32, (cl, cl), 1)
    eye_mask, tril, strict = ri == ci, ri >= ci, ri > ci
    eye = eye_mask.astype(_f32)

    def per_head(fn):
        return jnp.concatenate([fn(hh, slice(hh * hd, (hh + 1) * hd)).reshape(nc, cl, hd)
                                for hh in range(hp)], axis=0)

    def gate_col(lane_of_head):
        return lambda hh, cols: jnp.broadcast_to(
            jnp.sum(jnp.where(lane == lane_of_head + hh, gb, 0.0), axis=-1, keepdims=True), (tb, hd))

    q = per_head(lambda hh, cols: l2n(qf[:, cols]) * (hd ** -0.5))
    k = per_head(lambda hh, cols: l2n(kf[:, cols]))
    v = per_head(lambda hh, cols: vf[:, cols])
    gc = per_head(gate_col(h0))
    bt = per_head(gate_col(h0 + heads))
    gc_sq = gc[:, :, :cl]
    grow = jnp.sum(jnp.where(eye_mask, gc_sq, 0.0), axis=1, keepdims=True)
    decay = jnp.where(tril, jnp.exp(gc_sq - grow), 0.0)
    eg = jnp.exp(gc)
    kb = k * bt
    k16 = k.astype(_bf16)
    kq = _bdot_nt(jnp.concatenate([kb.astype(_bf16), q.astype(_bf16)], axis=1), k16)
    low = jnp.where(strict, kq[:, :cl] * decay, 0.0)
    qk = jnp.where(tril, kq[:, cl:] * decay, 0.0).astype(_bf16)
    tinv = _unit_lower_inverse(low, eye)
    rhs = jnp.concatenate([v * bt, kb * eg], axis=2)
    (sol,) = _bdot3_stacked([tinv], rhs)
    u = sol[:, :, :hd]
    gl = gc[:, cl - 1:cl, :]
    egl = jnp.exp(gl)
    wq = jnp.concatenate([sol[:, :, hd:].astype(_bf16), (q * eg).astype(_bf16)], axis=1)
    kd_t = jnp.swapaxes(k * jnp.exp(gl - gc), 1, 2).astype(_bf16)
    qkd = jnp.concatenate([qk, kd_t], axis=1)

    state = [s_ref[hh] for hh in range(hp)]
    outs = [[] for _ in range(hp)]
    for c in range(nc):
        for hh in range(hp):
            b = hh * nc + c
            s = state[hh]
            ws_qs = _dot(wq[b], s.astype(_bf16))
            vn16 = (u[b] - ws_qs[:cl]).astype(_bf16)
            r = _dot(qkd[b], vn16)
            outs[hh].append(ws_qs[cl:] + r[:cl])
            state[hh] = s * egl[b] + r[cl:]

    for hh in range(hp):
        cols = slice(hh * hd, (hh + 1) * hd)
        s_ref[hh] = state[hh]
        z = z_ref[:, cols].astype(_f32)
        o = jnp.concatenate(outs[hh], axis=0)
        o_ref[:, cols] = (_rms(o, gain_ref[...]) * (z * _sigmoid(z))).astype(o_ref.dtype)


def gdn_core(proj, gb, conv_w, out_gain, batch, seq, heads, tb=256, hp=4):
    n = proj.shape[0]
    tb, hp = min(tb, seq), min(hp, heads)
    nt = seq // tb
    hg = heads // hp
    width = hp * HEAD_DIM
    row = lambda b, g, t: b * nt + t
    col = lambda off: (lambda b, g, t: (row(b, g, t), off * hg + g))
    ccol = lambda off: (lambda b, g, t: (0, off * hg + g))
    blk = (tb, width)
    return pl.pallas_call(
        functools.partial(_gdn_core_kernel, heads=heads, hp=hp),
        grid=(batch, hg, nt),
        in_specs=[pl.BlockSpec(blk, col(0)), pl.BlockSpec(blk, col(1)), pl.BlockSpec(blk, col(2)),
                  pl.BlockSpec(blk, col(3)),
                  pl.BlockSpec((tb, HEAD_DIM), lambda b, g, t: (row(b, g, t), 0)),
                  pl.BlockSpec((4, width), ccol(0)), pl.BlockSpec((4, width), ccol(1)),
                  pl.BlockSpec((4, width), ccol(2)),
                  pl.BlockSpec((1, HEAD_DIM), lambda b, g, t: (0, 0))],
        out_specs=pl.BlockSpec(blk, lambda b, g, t: (row(b, g, t), g)),
        out_shape=jax.ShapeDtypeStruct((n, heads * HEAD_DIM), _bf16),
        scratch_shapes=[pltpu.VMEM((3, F32_SUBLANES, width), _f32),
                        pltpu.VMEM((hp, HEAD_DIM, HEAD_DIM), _f32)],
        compiler_params=_params("parallel", "parallel", "arbitrary"),
        name="gdn_core",
    )(proj, proj, proj, proj, gb, conv_w, conv_w, conv_w, out_gain.reshape(1, HEAD_DIM))


def _ffn_kernel(x_ref, xh_ref, g_ref, wg_ref, wu_ref, cg_ref, cu_ref, wd_ref, o_ref, xn_ref,
                *, blocks_per_seq):
    tm = x_ref.shape[0]
    halo = BF16_SUBLANES
    i, f = pl.program_id(0), pl.program_id(1)

    @pl.when(f == 0)
    def _():
        xn_ref[halo:, :] = _rms(x_ref[...], g_ref[...]).astype(xn_ref.dtype)
        prev = jnp.where(i % blocks_per_seq == 0, 0.0, _rms(xh_ref[...], g_ref[...]))
        xn_ref[:halo, :] = prev.astype(xn_ref.dtype)
        o_ref[...] = x_ref[...]

    xn = xn_ref[...]

    def conv_branch(w_ref, c_ref):
        u = _dot(xn, w_ref[...])
        cw = c_ref[...]
        return (cw[2:3] * u[halo:] + cw[1:2] * u[halo - 1:halo - 1 + tm]
                + cw[0:1] * u[halo - 2:halo - 2 + tm])

    gate = conv_branch(wg_ref, cg_ref)
    up = conv_branch(wu_ref, cu_ref)
    act = ((gate * _sigmoid(gate)) * up).astype(_bf16)
    o_ref[...] += _dot(act, wd_ref[...])


def conv_ffn(x, gain, w_up, conv_w, w_down, seq, tm=1024, tf=256):
    n, d = x.shape
    ff = w_down.shape[0]
    tm, tf = min(tm, seq), min(tf, ff)
    assert seq % tm == 0 and ff % tf == 0 and tm % BF16_SUBLANES == 0, (seq, tm, ff, tf)
    nf = ff // tf
    hb = tm // BF16_SUBLANES
    return pl.pallas_call(
        functools.partial(_ffn_kernel, blocks_per_seq=seq // tm),
        grid=(n // tm, nf),
        in_specs=[pl.BlockSpec((tm, d), lambda i, f: (i, 0)),
                  pl.BlockSpec((BF16_SUBLANES, d), lambda i, f: (jnp.maximum(i * hb - 1, 0), 0)),
                  pl.BlockSpec((1, d), lambda i, f: (0, 0)),
                  pl.BlockSpec((d, tf), lambda i, f: (0, f)),
                  pl.BlockSpec((d, tf), lambda i, f: (0, nf + f)),
                  pl.BlockSpec((FFN_CONV_W, tf), lambda i, f: (0, f)),
                  pl.BlockSpec((FFN_CONV_W, tf), lambda i, f: (0, nf + f)),
                  pl.BlockSpec((tf, d), lambda i, f: (f, 0))],
        out_specs=pl.BlockSpec((tm, d), lambda i, f: (i, 0)),
        out_shape=jax.ShapeDtypeStruct((n, d), _f32),
        scratch_shapes=[pltpu.VMEM((BF16_SUBLANES + tm, d), _bf16)],
        compiler_params=_params("parallel", "arbitrary"),
        name="conv_ffn",
    )(x, x, gain.reshape(1, d), w_up, w_up, conv_w, conv_w, w_down)


def _sb_kernel(q_ref, k_ref, v_ref, o_ref, c_ref, acc_ref, *, scale, hp):
    tq = q_ref.shape[0]
    tk = SB_KEY_BLOCK
    hd = HEAD_DIM
    nsub = tq // tk
    i = pl.program_id(2)
    rr = lax.broadcasted_iota(jnp.int32, (tk, tk), 0)
    cc = lax.broadcasted_iota(jnp.int32, (tk, tk), 1)
    tri = (rr >= cc).astype(_bf16)
    c_ref[...] = jnp.zeros_like(c_ref)
    acc_ref[...] = jnp.zeros_like(acc_ref)

    def scores(row0, row1, kstart, masked, hh):
        cols = slice(hh * hd, (hh + 1) * hd)
        z = _dot_nt(q_ref[row0:row1, cols], k_ref[pl.ds(kstart, tk), cols]) * scale
        sp = _softplus(z)
        mask = None
        if masked:
            t_loc = lax.broadcasted_iota(jnp.int32, z.shape, 0)
            s_loc = lax.broadcasted_iota(jnp.int32, z.shape, 1)
            mask = s_loc < t_loc
            sp = jnp.where(mask, sp, 0.0)
        both = _dot(jnp.concatenate(_split_bf16(sp, 2), axis=0), tri)
        return z - (both[:row1 - row0] + both[row1 - row0:]), both[:row1 - row0, 0:1] + both[row1 - row0:, 0:1], mask

    def absorb(row0, row1, kstart, hh, z_minus_cum, total, mask):
        cols = slice(hh * hd, (hh + 1) * hd)
        c = c_ref[row0:row1, cols]
        a = jnp.exp(z_minus_cum - jnp.concatenate([c] * (tk // hd), axis=1))
        if mask is not None:
            a = jnp.where(mask, a, 0.0)
        acc_ref[row0:row1, cols] += _dot(a.astype(_bf16), v_ref[pl.ds(kstart, tk), cols])
        c_ref[row0:row1, cols] = c + total

    def process(tiles):
        pre = [[scores(*t, hh) for hh in range(hp)] for t in tiles]
        for t, per_head in zip(tiles, pre):
            for hh in range(hp):
                absorb(t[0], t[1], t[2], hh, *per_head[hh])

    process([(sub * tk, tq, pl.multiple_of(i * tq + sub * tk, tk), True) for sub in reversed(range(nsub))])

    def walk(row0, row1):
        def alive():
            return (jnp.min(c_ref[row0:row1, :]) <= SB_STICK_EXHAUSTED).astype(jnp.int32)

        def cond(carry):
            j, go = carry
            return jnp.logical_and(j >= 0, go > 0)

        def body(carry):
            j, _ = carry
            process([(row0, row1, pl.multiple_of(j * tk, tk), False)])
            return j - 1, alive()

        lax.while_loop(cond, body, (i * nsub - 1, alive()))

    groups = 2 if tq % (2 * BF16_SUBLANES) == 0 else 1
    for grp in range(groups):
        walk(grp * (tq // groups), (grp + 1) * (tq // groups))
    o_ref[...] = acc_ref[...].astype(o_ref.dtype)


def sb_attention(q, kv, batch, seq, tq=512, hp=4):
    n, d = q.shape
    heads = d // HEAD_DIM
    tq, hp = min(tq, seq), min(hp, heads)
    nq = seq // tq
    width = hp * HEAD_DIM
    return pl.pallas_call(
        functools.partial(_sb_kernel, scale=HEAD_DIM ** -0.5, hp=hp),
        grid=(batch, heads // hp, nq),
        in_specs=[pl.BlockSpec((tq, width), lambda b, g, i: (b * nq + i, g)),
                  pl.BlockSpec((seq, width), lambda b, g, i: (b, g), pipeline_mode=pl.Buffered(1)),
                  pl.BlockSpec((seq, width), lambda b, g, i: (b, heads // hp + g), pipeline_mode=pl.Buffered(1))],
        out_specs=pl.BlockSpec((tq, width), lambda b, g, i: (b * nq + i, g)),
        out_shape=jax.ShapeDtypeStruct((n, d), _bf16),
        scratch_shapes=[pltpu.VMEM((tq, width), _f32), pltpu.VMEM((tq, width), _f32)],
        compiler_params=_params("parallel", "parallel", "arbitrary"),
        name="sb_attention",
    )(q, kv, kv)


def kernel(x, a_norm, a_w_in, a_conv, a_log, a_dt_bias, a_out_norm, a_w_out, kv_norm, w_kv, k_norm, b_norm, b_w_q, q_norm, b_w_out, ffn_norm, ffn_w_up, ffn_conv, ffn_w_down):
    batch, seq, d = x.shape
    n = batch * seq
    n_a = a_w_in.shape[0]
    n_b = b_w_q.shape[0]
    gdn_heads = a_log.shape[1]
    bf = lambda w: w.astype(_bf16)
    xs = x.reshape(n, d)
    kv = None
    for layer in range(n_a + n_b):
        if layer < n_a:
            w_in = a_w_in[layer]
            proj = norm_matmul(xs, a_norm[layer], bf(w_in[:, :4 * d]), _bf16, tn=2048)
            gb = gdn_gates(xs, a_norm[layer], w_in[:, 4 * d:], a_log[layer], a_dt_bias[layer])
            og = gdn_core(proj, gb, a_conv[layer], a_out_norm[layer], batch, seq, gdn_heads)
            xs = matmul_residual(og, bf(a_w_out[layer]), xs)
        else:
            j = layer - n_a
            if j == 0:
                kv = norm_matmul(xs, kv_norm, bf(w_kv), _bf16, head_gain=k_norm, normed_cols=d)
            q = norm_matmul(xs, b_norm[j], bf(b_w_q[j]), _bf16, head_gain=q_norm[j])
            o = sb_attention(q, kv, batch, seq)
            xs = matmul_residual(o, bf(b_w_out[j]), xs)
        xs = conv_ffn(xs, ffn_norm[layer], bf(ffn_w_up[layer]), ffn_conv[layer], bf(ffn_w_down[layer]), seq)
    return xs.reshape(batch, seq, d)
```

```python
import functools

import jax
import jax.numpy as jnp
from jax import lax
from jax.experimental import pallas as pl
from jax.experimental.pallas import tpu as pltpu

EPS = 1e-6
HEAD_DIM = 128
GDN_CHUNK = 64
FFN_CONV_W = 3
F32_SUBLANES = 8
BF16_SUBLANES = 16
SB_KEY_BLOCK = 256
SB_STICK_EXHAUSTED = 104.0
VMEM_LIMIT_BYTES = 56 * 1024 * 1024

_f32 = jnp.float32
_bf16 = jnp.bfloat16


def _params(*semantics):
    return pltpu.CompilerParams(dimension_semantics=semantics, vmem_limit_bytes=VMEM_LIMIT_BYTES)


def _rms(x, gain):
    ms = jnp.mean(x * x, axis=-1, keepdims=True)
    return (x * lax.rsqrt(ms + EPS)) * gain


def _softplus(x):
    return jnp.maximum(x, 0.0) + jnp.log(1.0 + jnp.exp(-jnp.abs(x)))


def _sigmoid(x):
    return 1.0 / (1.0 + jnp.exp(-x))


def _dot(a, b):
    return jnp.dot(a, b, preferred_element_type=_f32)


def _dot_nt(a, b):
    return lax.dot_general(a, b, (((1,), (1,)), ((), ())), preferred_element_type=_f32)


def _dot_tn(a, b):
    return lax.dot_general(a, b, (((0,), (0,)), ((), ())), preferred_element_type=_f32)


def _split_bf16(x, pieces):
    out = []
    for _ in range(pieces):
        hi = x.astype(_bf16)
        out.append(hi)
        x = x - hi.astype(_f32)
    return out


def _norm_matmul_kernel(x_ref, g_ref, w_ref, *rest, head_norm):
    if head_norm:
        hg_ref, o_ref, xn_ref = rest
    else:
        o_ref, xn_ref = rest

    @pl.when(pl.program_id(1) == 0)
    def _():
        xn_ref[...] = _rms(x_ref[...], g_ref[...]).astype(xn_ref.dtype)

    y = _dot(xn_ref[...], w_ref[...])
    if head_norm:
        parts = []
        for h in range(y.shape[1] // HEAD_DIM):
            parts.append(_rms(y[:, h * HEAD_DIM:(h + 1) * HEAD_DIM], hg_ref[...]))
        y = jnp.concatenate(parts, axis=1)
    o_ref[...] = y.astype(o_ref.dtype)


def norm_matmul(x, gain, w, out_dtype, head_gain=None, tm=1024, tn=1024):
    n, d = x.shape
    nout = w.shape[1]
    tm, tn = min(tm, n), min(tn, nout)
    in_specs = [pl.BlockSpec((tm, d), lambda i, j: (i, 0)),
                pl.BlockSpec((1, d), lambda i, j: (0, 0)),
                pl.BlockSpec((d, tn), lambda i, j: (0, j))]
    args = [x, gain.reshape(1, d), w]
    if head_gain is not None:
        in_specs.append(pl.BlockSpec((1, HEAD_DIM), lambda i, j: (0, 0)))
        args.append(head_gain.reshape(1, HEAD_DIM))
    return pl.pallas_call(
        functools.partial(_norm_matmul_kernel, head_norm=head_gain is not None),
        grid=(n // tm, nout // tn),
        in_specs=in_specs,
        out_specs=pl.BlockSpec((tm, tn), lambda i, j: (i, j)),
        out_shape=jax.ShapeDtypeStruct((n, nout), out_dtype),
        scratch_shapes=[pltpu.VMEM((tm, d), _bf16)],
        compiler_params=_params("parallel", "arbitrary"),
        name="norm_matmul",
    )(*args)


def _matmul_residual_kernel(a_ref, w_ref, x_ref, o_ref):
    o_ref[...] = x_ref[...] + _dot(a_ref[...], w_ref[...])


def matmul_residual(a, w, x, tm=1024):
    n, k = a.shape
    d = w.shape[1]
    tm = min(tm, n)
    return pl.pallas_call(
        _matmul_residual_kernel,
        grid=(n // tm,),
        in_specs=[pl.BlockSpec((tm, k), lambda i: (i, 0)),
                  pl.BlockSpec((k, d), lambda i: (0, 0)),
                  pl.BlockSpec((tm, d), lambda i: (i, 0))],
        out_specs=pl.BlockSpec((tm, d), lambda i: (i, 0)),
        out_shape=jax.ShapeDtypeStruct((n, d), _f32),
        compiler_params=_params("parallel"),
        name="matmul_residual",
    )(a, w, x)


def _gdn_in_proj_kernel(x_ref, g_ref, w_ref, wab_ref, alog_ref, dtb_ref, tri_ref, o_ref, gb_ref, xn_ref, *, heads):
    @pl.when(pl.program_id(1) == 0)
    def _():
        xn = _rms(x_ref[...], g_ref[...]).astype(_bf16)
        xn_ref[...] = xn
        y = _dot(xn, wab_ref[...])
        g = -jnp.exp(alog_ref[...]) * _softplus(y + dtb_ref[...])
        beta = _sigmoid(y)
        tri = tri_ref[...]
        rows = tri.shape[0]
        pieces = _split_bf16(g, 3)
        gc = jnp.concatenate([sum(_dot(tri, p[r:r + rows]) for p in pieces)
                              for r in range(0, y.shape[0], rows)], axis=0)
        lane = lax.broadcasted_iota(jnp.int32, y.shape, 1)
        gb_ref[...] = jnp.where(lane < heads, gc, beta)

    o_ref[...] = _dot(xn_ref[...], w_ref[...]).astype(o_ref.dtype)


def gdn_in_proj(x, gain, w_in, a_log, dt_bias, tm=1024, tn=2048, tri_rows=256):
    n, d = x.shape
    heads = a_log.shape[0]
    nout = w_in.shape[1] - 2 * heads
    tm, tn = min(tm, n), min(tn, nout)
    tri_rows = min(tri_rows, tm)
    assert n % tm == 0 and nout % tn == 0 and tm % tri_rows == 0 and tri_rows % GDN_CHUNK == 0
    pad = lambda v: jnp.zeros((1, HEAD_DIM), _f32).at[0, :heads].set(v.astype(_f32))
    w_ab = jnp.zeros((d, HEAD_DIM), _bf16).at[:, :2 * heads].set(w_in[:, nout:].astype(_bf16))
    r, c = jnp.arange(tri_rows)[:, None], jnp.arange(tri_rows)[None, :]
    tri = ((r // GDN_CHUNK == c // GDN_CHUNK) & (r >= c)).astype(_bf16)
    const = lambda shape: pl.BlockSpec(shape, lambda i, j: (0, 0))
    return pl.pallas_call(
        functools.partial(_gdn_in_proj_kernel, heads=heads),
        grid=(n // tm, nout // tn),
        in_specs=[pl.BlockSpec((tm, d), lambda i, j: (i, 0)), const((1, d)),
                  pl.BlockSpec((d, tn), lambda i, j: (0, j)),
                  const((d, HEAD_DIM)), const((1, HEAD_DIM)), const((1, HEAD_DIM)), const((tri_rows, tri_rows))],
        out_specs=[pl.BlockSpec((tm, tn), lambda i, j: (i, j)), pl.BlockSpec((tm, HEAD_DIM), lambda i, j: (i, 0))],
        out_shape=[jax.ShapeDtypeStruct((n, nout), _bf16), jax.ShapeDtypeStruct((n, HEAD_DIM), _f32)],
        scratch_shapes=[pltpu.VMEM((tm, d), _bf16)],
        compiler_params=_params("parallel", "arbitrary"),
        name="gdn_in_proj",
    )(x, gain.reshape(1, d), w_in[:, :nout].astype(_bf16), w_ab, pad(a_log), pad(dt_bias), tri)


def _bdot(a, b):
    return jnp.einsum("cij,cjk->cik", a, b, preferred_element_type=_f32)


def _bdot_nt(a, b):
    return jnp.einsum("cid,cjd->cij", a, b, preferred_element_type=_f32)


def _bdot3_stacked(lhs_list, b):
    n = lhs_list[0].shape[1]
    bh, bl = _split_bf16(b, 2)
    his, los = zip(*(_split_bf16(a, 2) for a in lhs_list))
    m = len(lhs_list)
    big = _bdot(jnp.concatenate(his + los, axis=1), bh)
    sml = _bdot(jnp.concatenate(his, axis=1), bl)
    rows = lambda t, i: t[:, i * n:(i + 1) * n]
    return [rows(big, i) + (rows(big, m + i) + rows(sml, i)) for i in range(m)]


def _unit_lower_inverse(low, eye):
    x = eye - low
    (p,) = _bdot3_stacked([low], low)
    for _ in range(4):
        xp, p = _bdot3_stacked([x, p], p)
        x = x + xp
    (xp,) = _bdot3_stacked([x], p)
    return x + xp


def _gdn_core_kernel(q_ref, k_ref, v_ref, z_ref, gb_ref, cq_ref, ck_ref, cv_ref, gain_ref, o_ref,
                     halo_ref, s_ref, *, heads, hp):
    tb = q_ref.shape[0]
    hd = HEAD_DIM
    cl = GDN_CHUNK
    nc = tb // cl
    h0 = pl.program_id(1) * hp

    @pl.when(pl.program_id(2) == 0)
    def _():
        halo_ref[...] = jnp.zeros_like(halo_ref)
        s_ref[...] = jnp.zeros_like(s_ref)

    def conv_silu(x_ref, w_ref, idx):
        raw = x_ref[...].astype(_f32)
        ext = jnp.concatenate([halo_ref[idx], raw], axis=0)
        w = w_ref[...]
        y = w[3:4] * raw
        for d in range(1, 4):
            y = y + w[3 - d:4 - d] * ext[F32_SUBLANES - d:F32_SUBLANES - d + tb]
        halo_ref[idx] = raw[tb - F32_SUBLANES:]
        return y * _sigmoid(y)

    def l2n(x):
        return x * lax.rsqrt(jnp.sum(x * x, axis=-1, keepdims=True) + EPS)

    qf = conv_silu(q_ref, cq_ref, 0)
    kf = conv_silu(k_ref, ck_ref, 1)
    vf = conv_silu(v_ref, cv_ref, 2)
    gb = gb_ref[...]
    lane = lax.broadcasted_iota(jnp.int32, gb.shape, 1)
    ri = lax.broadcasted_iota(jnp.int32, (cl, cl), 0)
    ci = lax.broadcasted_iota(jnp.int32, (cl, cl), 1)
    eye_mask, tril, strict = ri == ci, ri >= ci, ri > ci
    eye = eye_mask.astype(_f32)

    def per_head(fn):
        return jnp.concatenate([fn(hh, slice(hh * hd, (hh + 1) * hd)).reshape(nc, cl, hd)
                                for hh in range(hp)], axis=0)

    def gate_col(lane_of_head):
        return lambda hh, cols: jnp.broadcast_to(
            jnp.sum(jnp.where(lane == lane_of_head + hh, gb, 0.0), axis=-1, keepdims=True), (tb, hd))

    q = per_head(lambda hh, cols: l2n(qf[:, cols]) * (hd ** -0.5))
    k = per_head(lambda hh, cols: l2n(kf[:, cols]))
    v = per_head(lambda hh, cols: vf[:, cols])
    gc = per_head(gate_col(h0))
    bt = per_head(gate_col(h0 + heads))
    gc_sq = gc[:, :, :cl]
    grow = jnp.sum(jnp.where(eye_mask, gc_sq, 0.0), axis=1, keepdims=True)
    decay = jnp.where(tril, jnp.exp(gc_sq - grow), 0.0)
    eg = jnp.exp(gc)
    kb = k * bt
    k16 = k.astype(_bf16)
    kq = _bdot_nt(jnp.concatenate([kb.astype(_bf16), q.astype(_bf16)], axis=1), k16)
    low = jnp.where(strict, kq[:, :cl] * decay, 0.0)
    qk = jnp.where(tril, kq[:, cl:] * decay, 0.0).astype(_bf16)
    tinv = _unit_lower_inverse(low, eye)
    rhs = jnp.concatenate([v * bt, kb * eg], axis=2)
    (sol,) = _bdot3_stacked([tinv], rhs)
    u = sol[:, :, :hd]
    gl = gc[:, cl - 1:cl, :]
    egl = jnp.exp(gl)
    wq = jnp.concatenate([sol[:, :, hd:].astype(_bf16), (q * eg).astype(_bf16)], axis=1)
    kd_t = jnp.swapaxes(k * jnp.exp(gl - gc), 1, 2).astype(_bf16)
    qkd = jnp.concatenate([qk, kd_t], axis=1)

    state = [s_ref[hh] for hh in range(hp)]
    outs = [[] for _ in range(hp)]
    for c in range(nc):
        for hh in range(hp):
            b = hh * nc + c
            s = state[hh]
            ws_qs = _dot(wq[b], s.astype(_bf16))
            vn16 = (u[b] - ws_qs[:cl]).astype(_bf16)
            r = _dot(qkd[b], vn16)
            outs[hh].append(ws_qs[cl:] + r[:cl])
            state[hh] = s * egl[b] + r[cl:]

    for hh in range(hp):
        cols = slice(hh * hd, (hh + 1) * hd)
        s_ref[hh] = state[hh]
        z = z_ref[:, cols].astype(_f32)
        o = jnp.concatenate(outs[hh], axis=0)
        o_ref[:, cols] = (_rms(o, gain_ref[...]) * (z * _sigmoid(z))).astype(o_ref.dtype)


def gdn_core(proj, gb, conv_w, out_gain, batch, seq, heads, tb=256, hp=4):
    n = proj.shape[0]
    tb, hp = min(tb, seq), min(hp, heads)
    nt = seq // tb
    hg = heads // hp
    width = hp * HEAD_DIM
    row = lambda b, g, t: b * nt + t
    col = lambda off: (lambda b, g, t: (row(b, g, t), off * hg + g))
    ccol = lambda off: (lambda b, g, t: (0, off * hg + g))
    blk = (tb, width)
    return pl.pallas_call(
        functools.partial(_gdn_core_kernel, heads=heads, hp=hp),
        grid=(batch, hg, nt),
        in_specs=[pl.BlockSpec(blk, col(0)), pl.BlockSpec(blk, col(1)), pl.BlockSpec(blk, col(2)),
                  pl.BlockSpec(blk, col(3)),
                  pl.BlockSpec((tb, HEAD_DIM), lambda b, g, t: (row(b, g, t), 0)),
                  pl.BlockSpec((4, width), ccol(0)), pl.BlockSpec((4, width), ccol(1)),
                  pl.BlockSpec((4, width), ccol(2)),
                  pl.BlockSpec((1, HEAD_DIM), lambda b, g, t: (0, 0))],
        out_specs=pl.BlockSpec(blk, lambda b, g, t: (row(b, g, t), g)),
        out_shape=jax.ShapeDtypeStruct((n, heads * HEAD_DIM), _bf16),
        scratch_shapes=[pltpu.VMEM((3, F32_SUBLANES, width), _f32),
                        pltpu.VMEM((hp, HEAD_DIM, HEAD_DIM), _f32)],
        compiler_params=_params("parallel", "parallel", "arbitrary"),
        name="gdn_core",
    )(proj, proj, proj, proj, gb, conv_w, conv_w, conv_w, out_gain.reshape(1, HEAD_DIM))


def _ffn_kernel(x_ref, xh_ref, g_ref, wg_ref, wu_ref, cg_ref, cu_ref, wd_ref, o_ref, xn_ref,
                *, blocks_per_seq):
    tm = x_ref.shape[0]
    halo = BF16_SUBLANES
    i, f = pl.program_id(0), pl.program_id(1)

    @pl.when(f == 0)
    def _():
        xn_ref[halo:, :] = _rms(x_ref[...], g_ref[...]).astype(xn_ref.dtype)
        prev = jnp.where(i % blocks_per_seq == 0, 0.0, _rms(xh_ref[...], g_ref[...]))
        xn_ref[:halo, :] = prev.astype(xn_ref.dtype)
        o_ref[...] = x_ref[...]

    xn = xn_ref[...]

    def conv_branch(w_ref, c_ref):
        u = _dot(xn, w_ref[...])
        cw = c_ref[...]
        return (cw[2:3] * u[halo:] + cw[1:2] * u[halo - 1:halo - 1 + tm]
                + cw[0:1] * u[halo - 2:halo - 2 + tm])

    gate = conv_branch(wg_ref, cg_ref)
    up = conv_branch(wu_ref, cu_ref)
    act = ((gate * _sigmoid(gate)) * up).astype(_bf16)
    o_ref[...] += _dot(act, wd_ref[...])


def conv_ffn(x, gain, w_up, conv_w, w_down, seq, tm=1024, tf=256):
    n, d = x.shape
    ff = w_down.shape[0]
    tm, tf = min(tm, seq), min(tf, ff)
    assert seq % tm == 0 and ff % tf == 0 and tm % BF16_SUBLANES == 0, (seq, tm, ff, tf)
    nf = ff // tf
    hb = tm // BF16_SUBLANES
    return pl.pallas_call(
        functools.partial(_ffn_kernel, blocks_per_seq=seq // tm),
        grid=(n // tm, nf),
        in_specs=[pl.BlockSpec((tm, d), lambda i, f: (i, 0)),
                  pl.BlockSpec((BF16_SUBLANES, d), lambda i, f: (jnp.maximum(i * hb - 1, 0), 0)),
                  pl.BlockSpec((1, d), lambda i, f: (0, 0)),
                  pl.BlockSpec((d, tf), lambda i, f: (0, f)),
                  pl.BlockSpec((d, tf), lambda i, f: (0, nf + f)),
                  pl.BlockSpec((FFN_CONV_W, tf), lambda i, f: (0, f)),
                  pl.BlockSpec((FFN_CONV_W, tf), lambda i, f: (0, nf + f)),
                  pl.BlockSpec((tf, d), lambda i, f: (f, 0))],
        out_specs=pl.BlockSpec((tm, d), lambda i, f: (i, 0)),
        out_shape=jax.ShapeDtypeStruct((n, d), _f32),
        scratch_shapes=[pltpu.VMEM((BF16_SUBLANES + tm, d), _bf16)],
        compiler_params=_params("parallel", "arbitrary"),
        name="conv_ffn",
    )(x, x, gain.reshape(1, d), w_up, w_up, conv_w, conv_w, w_down)


def _sb_kernel(q_ref, k_ref, v_ref, o_ref, c_ref, acc_ref, *, scale, hp):
    tq = q_ref.shape[0]
    tk = SB_KEY_BLOCK
    hd = HEAD_DIM
    nsub = tq // tk
    i = pl.program_id(2)
    rr = lax.broadcasted_iota(jnp.int32, (tk, tk), 0)
    cc = lax.broadcasted_iota(jnp.int32, (tk, tk), 1)
    tri = (rr >= cc).astype(_bf16)
    c_ref[...] = jnp.zeros_like(c_ref)
    acc_ref[...] = jnp.zeros_like(acc_ref)

    def scores(row0, row1, kstart, masked, hh):
        cols = slice(hh * hd, (hh + 1) * hd)
        z = _dot_nt(q_ref[row0:row1, cols], k_ref[pl.ds(kstart, tk), cols]) * scale
        sp = _softplus(z)
        mask = None
        if masked:
            t_loc = lax.broadcasted_iota(jnp.int32, z.shape, 0)
            s_loc = lax.broadcasted_iota(jnp.int32, z.shape, 1)
            mask = s_loc < t_loc
            sp = jnp.where(mask, sp, 0.0)
        both = _dot(jnp.concatenate(_split_bf16(sp, 2), axis=0), tri)
        return z - (both[:row1 - row0] + both[row1 - row0:]), both[:row1 - row0, 0:1] + both[row1 - row0:, 0:1], mask

    def absorb(row0, row1, kstart, hh, z_minus_cum, total, mask):
        cols = slice(hh * hd, (hh + 1) * hd)
        c = c_ref[row0:row1, cols]
        a = jnp.exp(z_minus_cum - jnp.concatenate([c] * (tk // hd), axis=1))
        if mask is not None:
            a = jnp.where(mask, a, 0.0)
        acc_ref[row0:row1, cols] += _dot(a.astype(_bf16), v_ref[pl.ds(kstart, tk), cols])
        c_ref[row0:row1, cols] = c + total

    def process(tiles):
        pre = [[scores(*t, hh) for hh in range(hp)] for t in tiles]
        for t, per_head in zip(tiles, pre):
            for hh in range(hp):
                absorb(t[0], t[1], t[2], hh, *per_head[hh])

    process([(sub * tk, tq, pl.multiple_of(i * tq + sub * tk, tk), True) for sub in reversed(range(nsub))])

    def walk(row0, row1):
        def alive():
            return (jnp.min(c_ref[row0:row1, :]) <= SB_STICK_EXHAUSTED).astype(jnp.int32)

        def cond(carry):
            j, go = carry
            return jnp.logical_and(j >= 0, go > 0)

        def body(carry):
            j, _ = carry
            process([(row0, row1, pl.multiple_of(j * tk, tk), False)])
            return j - 1, alive()

        lax.while_loop(cond, body, (i * nsub - 1, alive()))

    groups = 2 if tq % (2 * BF16_SUBLANES) == 0 else 1
    for grp in range(groups):
        walk(grp * (tq // groups), (grp + 1) * (tq // groups))
    o_ref[...] = acc_ref[...].astype(o_ref.dtype)


def sb_attention(q, k, v, batch, seq, tq=512, hp=4):
    n, d = q.shape
    heads = d // HEAD_DIM
    tq, hp = min(tq, seq), min(hp, heads)
    nq = seq // tq
    width = hp * HEAD_DIM
    return pl.pallas_call(
        functools.partial(_sb_kernel, scale=HEAD_DIM ** -0.5, hp=hp),
        grid=(batch, heads // hp, nq),
        in_specs=[pl.BlockSpec((tq, width), lambda b, g, i: (b * nq + i, g)),
                  pl.BlockSpec((seq, width), lambda b, g, i: (b, g), pipeline_mode=pl.Buffered(1)),
                  pl.BlockSpec((seq, width), lambda b, g, i: (b, g), pipeline_mode=pl.Buffered(1))],
        out_specs=pl.BlockSpec((tq, width), lambda b, g, i: (b * nq + i, g)),
        out_shape=jax.ShapeDtypeStruct((n, d), _bf16),
        scratch_shapes=[pltpu.VMEM((tq, width), _f32), pltpu.VMEM((tq, width), _f32)],
        compiler_params=_params("parallel", "parallel", "arbitrary"),
        name="sb_attention",
    )(q, k, v)


def kernel(x, a_norm, a_w_in, a_conv, a_log, a_dt_bias, a_out_norm, a_w_out, kv_norm, w_kv, k_norm, b_norm, b_w_q, q_norm, b_w_out, ffn_norm, ffn_w_up, ffn_conv, ffn_w_down):
    batch, seq, d = x.shape
    n = batch * seq
    n_a = a_w_in.shape[0]
    n_b = b_w_q.shape[0]
    gdn_heads = a_log.shape[1]
    bf = lambda w: w.astype(_bf16)
    xs = x.reshape(n, d)
    k_s = v_s = None
    for layer in range(n_a + n_b):
        if layer < n_a:
            proj, gb = gdn_in_proj(xs, a_norm[layer], a_w_in[layer], a_log[layer], a_dt_bias[layer])
            og = gdn_core(proj, gb, a_conv[layer], a_out_norm[layer], batch, seq, gdn_heads)
            xs = matmul_residual(og, bf(a_w_out[layer]), xs)
        else:
            j = layer - n_a
            if j == 0:
                k_s = norm_matmul(xs, kv_norm, bf(w_kv[:, :d]), _bf16, head_gain=k_norm)
                v_s = norm_matmul(xs, kv_norm, bf(w_kv[:, d:]), _bf16)
            q = norm_matmul(xs, b_norm[j], bf(b_w_q[j]), _bf16, head_gain=q_norm[j])
            o = sb_attention(q, k_s, v_s, batch, seq)
            xs = matmul_residual(o, bf(b_w_out[j]), xs)
        xs = conv_ffn(xs, ffn_norm[layer], bf(ffn_w_up[layer]), ffn_conv[layer], bf(ffn_w_down[layer]), seq)
    return xs.reshape(batch, seq, d)
```

```python
import functools

import jax
import jax.numpy as jnp
from jax import lax
from jax.experimental import pallas as pl
from jax.experimental.pallas import tpu as pltpu

EPS = 1e-6
HEAD_DIM = 128
GDN_CHUNK = 64
FFN_CONV_W = 3
F32_SUBLANES = 8
BF16_SUBLANES = 16
SB_KEY_BLOCK = 256
SB_STICK_EXHAUSTED = 104.0
VMEM_LIMIT_BYTES = 56 * 1024 * 1024

_f32 = jnp.float32
_bf16 = jnp.bfloat16


def _params(*semantics):
    return pltpu.CompilerParams(dimension_semantics=semantics, vmem_limit_bytes=VMEM_LIMIT_BYTES)


def _rms(x, gain):
    ms = jnp.mean(x * x, axis=-1, keepdims=True)
    return (x * lax.rsqrt(ms + EPS)) * gain


def _softplus(x):
    return jnp.maximum(x, 0.0) + jnp.log(1.0 + jnp.exp(-jnp.abs(x)))


def _sigmoid(x):
    return 1.0 / (1.0 + jnp.exp(-x))


def _dot(a, b):
    return jnp.dot(a, b, preferred_element_type=_f32)


def _dot_nt(a, b):
    return lax.dot_general(a, b, (((1,), (1,)), ((), ())), preferred_element_type=_f32)


def _dot_tn(a, b):
    return lax.dot_general(a, b, (((0,), (0,)), ((), ())), preferred_element_type=_f32)


def _split_bf16(x, pieces):
    out = []
    for _ in range(pieces):
        hi = x.astype(_bf16)
        out.append(hi)
        x = x - hi.astype(_f32)
    return out


def _norm_matmul_kernel(x_ref, g_ref, w_ref, *rest, head_norm):
    if head_norm:
        hg_ref, o_ref, xn_ref = rest
    else:
        o_ref, xn_ref = rest

    @pl.when(pl.program_id(1) == 0)
    def _():
        xn_ref[...] = _rms(x_ref[...], g_ref[...]).astype(xn_ref.dtype)

    y = _dot(xn_ref[...], w_ref[...])
    if head_norm:
        parts = []
        for h in range(y.shape[1] // HEAD_DIM):
            parts.append(_rms(y[:, h * HEAD_DIM:(h + 1) * HEAD_DIM], hg_ref[...]))
        y = jnp.concatenate(parts, axis=1)
    o_ref[...] = y.astype(o_ref.dtype)


def norm_matmul(x, gain, w, out_dtype, head_gain=None, tm=1024, tn=1024):
    n, d = x.shape
    nout = w.shape[1]
    tm, tn = min(tm, n), min(tn, nout)
    in_specs = [pl.BlockSpec((tm, d), lambda i, j: (i, 0)),
                pl.BlockSpec((1, d), lambda i, j: (0, 0)),
                pl.BlockSpec((d, tn), lambda i, j: (0, j))]
    args = [x, gain.reshape(1, d), w]
    if head_gain is not None:
        in_specs.append(pl.BlockSpec((1, HEAD_DIM), lambda i, j: (0, 0)))
        args.append(head_gain.reshape(1, HEAD_DIM))
    return pl.pallas_call(
        functools.partial(_norm_matmul_kernel, head_norm=head_gain is not None),
        grid=(n // tm, nout // tn),
        in_specs=in_specs,
        out_specs=pl.BlockSpec((tm, tn), lambda i, j: (i, j)),
        out_shape=jax.ShapeDtypeStruct((n, nout), out_dtype),
        scratch_shapes=[pltpu.VMEM((tm, d), _bf16)],
        compiler_params=_params("parallel", "arbitrary"),
        name="norm_matmul",
    )(*args)


def _matmul_residual_kernel(a_ref, w_ref, x_ref, o_ref):
    o_ref[...] = x_ref[...] + _dot(a_ref[...], w_ref[...])


def matmul_residual(a, w, x, tm=1024):
    n, k = a.shape
    d = w.shape[1]
    tm = min(tm, n)
    return pl.pallas_call(
        _matmul_residual_kernel,
        grid=(n // tm,),
        in_specs=[pl.BlockSpec((tm, k), lambda i: (i, 0)),
                  pl.BlockSpec((k, d), lambda i: (0, 0)),
                  pl.BlockSpec((tm, d), lambda i: (i, 0))],
        out_specs=pl.BlockSpec((tm, d), lambda i: (i, 0)),
        out_shape=jax.ShapeDtypeStruct((n, d), _f32),
        compiler_params=_params("parallel"),
        name="matmul_residual",
    )(a, w, x)


def _gdn_in_proj_kernel(x_ref, g_ref, w_ref, wab_ref, alog_ref, dtb_ref, tri_ref, o_ref, gb_ref, xn_ref, *, heads):
    @pl.when(pl.program_id(1) == 0)
    def _():
        xn = _rms(x_ref[...], g_ref[...]).astype(_bf16)
        xn_ref[...] = xn
        y = _dot(xn, wab_ref[...])
        g = -jnp.exp(alog_ref[...]) * _softplus(y + dtb_ref[...])
        beta = _sigmoid(y)
        tri = tri_ref[...]
        rows = tri.shape[0]
        pieces = _split_bf16(g, 3)
        gc = jnp.concatenate([sum(_dot(tri, p[r:r + rows]) for p in pieces)
                              for r in range(0, y.shape[0], rows)], axis=0)
        lane = lax.broadcasted_iota(jnp.int32, y.shape, 1)
        gb_ref[...] = jnp.where(lane < heads, gc, beta)

    o_ref[...] = _dot(xn_ref[...], w_ref[...]).astype(o_ref.dtype)


def gdn_in_proj(x, gain, w_in, a_log, dt_bias, tm=1024, tn=2048, tri_rows=256):
    n, d = x.shape
    heads = a_log.shape[0]
    nout = w_in.shape[1] - 2 * heads
    tm, tn = min(tm, n), min(tn, nout)
    tri_rows = min(tri_rows, tm)
    assert n % tm == 0 and nout % tn == 0 and tm % tri_rows == 0 and tri_rows % GDN_CHUNK == 0
    pad = lambda v: jnp.zeros((1, HEAD_DIM), _f32).at[0, :heads].set(v.astype(_f32))
    w_ab = jnp.zeros((d, HEAD_DIM), _bf16).at[:, :2 * heads].set(w_in[:, nout:].astype(_bf16))
    r, c = jnp.arange(tri_rows)[:, None], jnp.arange(tri_rows)[None, :]
    tri = ((r // GDN_CHUNK == c // GDN_CHUNK) & (r >= c)).astype(_bf16)
    const = lambda shape: pl.BlockSpec(shape, lambda i, j: (0, 0))
    return pl.pallas_call(
        functools.partial(_gdn_in_proj_kernel, heads=heads),
        grid=(n // tm, nout // tn),
        in_specs=[pl.BlockSpec((tm, d), lambda i, j: (i, 0)), const((1, d)),
                  pl.BlockSpec((d, tn), lambda i, j: (0, j)),
                  const((d, HEAD_DIM)), const((1, HEAD_DIM)), const((1, HEAD_DIM)), const((tri_rows, tri_rows))],
        out_specs=[pl.BlockSpec((tm, tn), lambda i, j: (i, j)), pl.BlockSpec((tm, HEAD_DIM), lambda i, j: (i, 0))],
        out_shape=[jax.ShapeDtypeStruct((n, nout), _bf16), jax.ShapeDtypeStruct((n, HEAD_DIM), _f32)],
        scratch_shapes=[pltpu.VMEM((tm, d), _bf16)],
        compiler_params=_params("parallel", "arbitrary"),
        name="gdn_in_proj",
    )(x, gain.reshape(1, d), w_in[:, :nout].astype(_bf16), w_ab, pad(a_log), pad(dt_bias), tri)


def _bdot(a, b):
    return jnp.einsum("cij,cjk->cik", a, b, preferred_element_type=_f32)


def _bdot_nt(a, b):
    return jnp.einsum("cid,cjd->cij", a, b, preferred_element_type=_f32)


def _bdot3_stacked(lhs_list, b):
    n = lhs_list[0].shape[1]
    bh, bl = _split_bf16(b, 2)
    his, los = zip(*(_split_bf16(a, 2) for a in lhs_list))
    m = len(lhs_list)
    big = _bdot(jnp.concatenate(his + los, axis=1), bh)
    sml = _bdot(jnp.concatenate(his, axis=1), bl)
    rows = lambda t, i: t[:, i * n:(i + 1) * n]
    return [rows(big, i) + (rows(big, m + i) + rows(sml, i)) for i in range(m)]


def _unit_lower_inverse(low, eye):
    x = eye - low
    (p,) = _bdot3_stacked([low], low)
    for _ in range(4):
        xp, p = _bdot3_stacked([x, p], p)
        x = x + xp
    (xp,) = _bdot3_stacked([x], p)
    return x + xp


def _gdn_core_kernel(q_ref, k_ref, v_ref, z_ref, gb_ref, cq_ref, ck_ref, cv_ref, gain_ref, o_ref,
                     halo_ref, s_ref, *, heads, hp):
    tb = q_ref.shape[0]
    hd = HEAD_DIM
    cl = GDN_CHUNK
    nc = tb // cl
    h0 = pl.program_id(1) * hp

    @pl.when(pl.program_id(2) == 0)
    def _():
        halo_ref[...] = jnp.zeros_like(halo_ref)
        s_ref[...] = jnp.zeros_like(s_ref)

    def conv_silu(x_ref, w_ref, idx):
        raw = x_ref[...].astype(_f32)
        ext = jnp.concatenate([halo_ref[idx], raw], axis=0)
        w = w_ref[...]
        y = w[3:4] * raw
        for d in range(1, 4):
            y = y + w[3 - d:4 - d] * ext[F32_SUBLANES - d:F32_SUBLANES - d + tb]
        halo_ref[idx] = raw[tb - F32_SUBLANES:]
        return y * _sigmoid(y)

    def l2n(x):
        return x * lax.rsqrt(jnp.sum(x * x, axis=-1, keepdims=True) + EPS)

    qf = conv_silu(q_ref, cq_ref, 0)
    kf = conv_silu(k_ref, ck_ref, 1)
    vf = conv_silu(v_ref, cv_ref, 2)
    gb = gb_ref[...]
    lane = lax.broadcasted_iota(jnp.int32, gb.shape, 1)
    ri = lax.broadcasted_iota(jnp.int32, (cl, cl), 0)
    ci = lax.broadcasted_iota(jnp.int32, (cl, cl), 1)
    eye_mask, tril, strict = ri == ci, ri >= ci, ri > ci
    eye = eye_mask.astype(_f32)

    def per_head(fn):
        return jnp.concatenate([fn(hh, slice(hh * hd, (hh + 1) * hd)).reshape(nc, cl, hd)
                                for hh in range(hp)], axis=0)

    def gate_col(lane_of_head):
        return lambda hh, cols: jnp.broadcast_to(
            jnp.sum(jnp.where(lane == lane_of_head + hh, gb, 0.0), axis=-1, keepdims=True), (tb, hd))

    q = per_head(lambda hh, cols: l2n(qf[:, cols]) * (hd ** -0.5))
    k = per_head(lambda hh, cols: l2n(kf[:, cols]))
    v = per_head(lambda hh, cols: vf[:, cols])
    gc = per_head(gate_col(h0))
    bt = per_head(gate_col(h0 + heads))
    gc_sq = gc[:, :, :cl]
    grow = jnp.sum(jnp.where(eye_mask, gc_sq, 0.0), axis=1, keepdims=True)
    decay = jnp.where(tril, jnp.exp(gc_sq - grow), 0.0)
    eg = jnp.exp(gc)
    kb = k * bt
    k16 = k.astype(_bf16)
    kq = _bdot_nt(jnp.concatenate([kb.astype(_bf16), q.astype(_bf16)], axis=1), k16)
    low = jnp.where(strict, kq[:, :cl] * decay, 0.0)
    qk = jnp.where(tril, kq[:, cl:] * decay, 0.0).astype(_bf16)
    tinv = _unit_lower_inverse(low, eye)
    rhs = jnp.concatenate([v * bt, kb * eg], axis=2)
    (sol,) = _bdot3_stacked([tinv], rhs)
    u = sol[:, :, :hd]
    gl = gc[:, cl - 1:cl, :]
    egl = jnp.exp(gl)
    wq = jnp.concatenate([sol[:, :, hd:].astype(_bf16), (q * eg).astype(_bf16)], axis=1)
    kd_t = jnp.swapaxes(k * jnp.exp(gl - gc), 1, 2).astype(_bf16)
    qkd = jnp.concatenate([qk, kd_t], axis=1)

    state = [s_ref[hh] for hh in range(hp)]
    outs = [[] for _ in range(hp)]
    for c in range(nc):
        for hh in range(hp):
            b = hh * nc + c
            s = state[hh]
            ws_qs = _dot(wq[b], s.astype(_bf16))
            vn16 = (u[b] - ws_qs[:cl]).astype(_bf16)
            r = _dot(qkd[b], vn16)
            outs[hh].append(ws_qs[cl:] + r[:cl])
            state[hh] = s * egl[b] + r[cl:]

    for hh in range(hp):
        cols = slice(hh * hd, (hh + 1) * hd)
        s_ref[hh] = state[hh]
        z = z_ref[:, cols].astype(_f32)
        o = jnp.concatenate(outs[hh], axis=0)
        o_ref[:, cols] = (_rms(o, gain_ref[...]) * (z * _sigmoid(z))).astype(o_ref.dtype)


def gdn_core(proj, gb, conv_w, out_gain, batch, seq, heads, tb=256, hp=8):
    n = proj.shape[0]
    tb, hp = min(tb, seq), min(hp, heads)
    nt = seq // tb
    hg = heads // hp
    width = hp * HEAD_DIM
    row = lambda b, g, t: b * nt + t
    col = lambda off: (lambda b, g, t: (row(b, g, t), off * hg + g))
    ccol = lambda off: (lambda b, g, t: (0, off * hg + g))
    blk = (tb, width)
    return pl.pallas_call(
        functools.partial(_gdn_core_kernel, heads=heads, hp=hp),
        grid=(batch, hg, nt),
        in_specs=[pl.BlockSpec(blk, col(0)), pl.BlockSpec(blk, col(1)), pl.BlockSpec(blk, col(2)),
                  pl.BlockSpec(blk, col(3)),
                  pl.BlockSpec((tb, HEAD_DIM), lambda b, g, t: (row(b, g, t), 0)),
                  pl.BlockSpec((4, width), ccol(0)), pl.BlockSpec((4, width), ccol(1)),
                  pl.BlockSpec((4, width), ccol(2)),
                  pl.BlockSpec((1, HEAD_DIM), lambda b, g, t: (0, 0))],
        out_specs=pl.BlockSpec(blk, lambda b, g, t: (row(b, g, t), g)),
        out_shape=jax.ShapeDtypeStruct((n, heads * HEAD_DIM), _bf16),
        scratch_shapes=[pltpu.VMEM((3, F32_SUBLANES, width), _f32),
                        pltpu.VMEM((hp, HEAD_DIM, HEAD_DIM), _f32)],
        compiler_params=_params("parallel", "parallel", "arbitrary"),
        name="gdn_core",
    )(proj, proj, proj, proj, gb, conv_w, conv_w, conv_w, out_gain.reshape(1, HEAD_DIM))


def _ffn_kernel(x_ref, xh_ref, g_ref, wg_ref, wu_ref, cg_ref, cu_ref, wd_ref, o_ref, xn_ref,
                *, blocks_per_seq):
    tm = x_ref.shape[0]
    halo = BF16_SUBLANES
    i, f = pl.program_id(0), pl.program_id(1)

    @pl.when(f == 0)
    def _():
        xn_ref[halo:, :] = _rms(x_ref[...], g_ref[...]).astype(xn_ref.dtype)
        prev = jnp.where(i % blocks_per_seq == 0, 0.0, _rms(xh_ref[...], g_ref[...]))
        xn_ref[:halo, :] = prev.astype(xn_ref.dtype)
        o_ref[...] = x_ref[...]

    xn = xn_ref[...]

    def conv_branch(w_ref, c_ref):
        u = _dot(xn, w_ref[...])
        cw = c_ref[...]
        return (cw[2:3] * u[halo:] + cw[1:2] * u[halo - 1:halo - 1 + tm]
                + cw[0:1] * u[halo - 2:halo - 2 + tm])

    gate = conv_branch(wg_ref, cg_ref)
    up = conv_branch(wu_ref, cu_ref)
    act = ((gate * _sigmoid(gate)) * up).astype(_bf16)
    o_ref[...] += _dot(act, wd_ref[...])


def conv_ffn(x, gain, w_up, conv_w, w_down, seq, tm=1024, tf=256):
    n, d = x.shape
    ff = w_down.shape[0]
    tm, tf = min(tm, seq), min(tf, ff)
    assert seq % tm == 0 and ff % tf == 0 and tm % BF16_SUBLANES == 0, (seq, tm, ff, tf)
    nf = ff // tf
    hb = tm // BF16_SUBLANES
    return pl.pallas_call(
        functools.partial(_ffn_kernel, blocks_per_seq=seq // tm),
        grid=(n // tm, nf),
        in_specs=[pl.BlockSpec((tm, d), lambda i, f: (i, 0)),
                  pl.BlockSpec((BF16_SUBLANES, d), lambda i, f: (jnp.maximum(i * hb - 1, 0), 0)),
                  pl.BlockSpec((1, d), lambda i, f: (0, 0)),
                  pl.BlockSpec((d, tf), lambda i, f: (0, f)),
                  pl.BlockSpec((d, tf), lambda i, f: (0, nf + f)),
                  pl.BlockSpec((FFN_CONV_W, tf), lambda i, f: (0, f)),
                  pl.BlockSpec((FFN_CONV_W, tf), lambda i, f: (0, nf + f)),
                  pl.BlockSpec((tf, d), lambda i, f: (f, 0))],
        out_specs=pl.BlockSpec((tm, d), lambda i, f: (i, 0)),
        out_shape=jax.ShapeDtypeStruct((n, d), _f32),
        scratch_shapes=[pltpu.VMEM((BF16_SUBLANES + tm, d), _bf16)],
        compiler_params=_params("parallel", "arbitrary"),
        name="conv_ffn",
    )(x, x, gain.reshape(1, d), w_up, w_up, conv_w, conv_w, w_down)


def _sb_kernel(q_ref, k_ref, v_ref, o_ref, c_ref, acc_ref, *, scale, hp):
    tq = q_ref.shape[0]
    tk = SB_KEY_BLOCK
    hd = HEAD_DIM
    nsub = tq // tk
    i = pl.program_id(2)
    rr = lax.broadcasted_iota(jnp.int32, (tk, tk), 0)
    cc = lax.broadcasted_iota(jnp.int32, (tk, tk), 1)
    tri = (rr >= cc).astype(_bf16)
    c_ref[...] = jnp.zeros_like(c_ref)
    acc_ref[...] = jnp.zeros_like(acc_ref)

    def scores(row0, row1, kstart, masked, hh):
        cols = slice(hh * hd, (hh + 1) * hd)
        z = _dot_nt(q_ref[row0:row1, cols], k_ref[pl.ds(kstart, tk), cols]) * scale
        sp = _softplus(z)
        mask = None
        if masked:
            t_loc = lax.broadcasted_iota(jnp.int32, z.shape, 0)
            s_loc = lax.broadcasted_iota(jnp.int32, z.shape, 1)
            mask = s_loc < t_loc
            sp = jnp.where(mask, sp, 0.0)
        both = _dot(jnp.concatenate(_split_bf16(sp, 2), axis=0), tri)
        return z - (both[:row1 - row0] + both[row1 - row0:]), both[:row1 - row0, 0:1] + both[row1 - row0:, 0:1], mask

    def absorb(row0, row1, kstart, hh, z_minus_cum, total, mask):
        cols = slice(hh * hd, (hh + 1) * hd)
        c = c_ref[row0:row1, cols]
        a = jnp.exp(z_minus_cum - jnp.concatenate([c] * (tk // hd), axis=1))
        if mask is not None:
            a = jnp.where(mask, a, 0.0)
        acc_ref[row0:row1, cols] += _dot(a.astype(_bf16), v_ref[pl.ds(kstart, tk), cols])
        c_ref[row0:row1, cols] = c + total

    def process(tiles):
        pre = [[scores(*t, hh) for hh in range(hp)] for t in tiles]
        for t, per_head in zip(tiles, pre):
            for hh in range(hp):
                absorb(t[0], t[1], t[2], hh, *per_head[hh])

    process([(sub * tk, tq, pl.multiple_of(i * tq + sub * tk, tk), True) for sub in reversed(range(nsub))])

    def walk(row0, row1):
        def alive():
            return (jnp.min(c_ref[row0:row1, :]) <= SB_STICK_EXHAUSTED).astype(jnp.int32)

        def cond(carry):
            j, go = carry
            return jnp.logical_and(j >= 0, go > 0)

        def body(carry):
            j, _ = carry
            process([(row0, row1, pl.multiple_of(j * tk, tk), False)])
            return j - 1, alive()

        lax.while_loop(cond, body, (i * nsub - 1, alive()))

    groups = 2 if tq % (2 * BF16_SUBLANES) == 0 else 1
    for grp in range(groups):
        walk(grp * (tq // groups), (grp + 1) * (tq // groups))
    o_ref[...] = acc_ref[...].astype(o_ref.dtype)


def sb_attention(q, k, v, batch, seq, tq=512, hp=4):
    n, d = q.shape
    heads = d // HEAD_DIM
    tq, hp = min(tq, seq), min(hp, heads)
    nq = seq // tq
    width = hp * HEAD_DIM
    return pl.pallas_call(
        functools.partial(_sb_kernel, scale=HEAD_DIM ** -0.5, hp=hp),
        grid=(batch, heads // hp, nq),
        in_specs=[pl.BlockSpec((tq, width), lambda b, g, i: (b * nq + i, g)),
                  pl.BlockSpec((seq, width), lambda b, g, i: (b, g), pipeline_mode=pl.Buffered(1)),
                  pl.BlockSpec((seq, width), lambda b, g, i: (b, g), pipeline_mode=pl.Buffered(1))],
        out_specs=pl.BlockSpec((tq, width), lambda b, g, i: (b * nq + i, g)),
        out_shape=jax.ShapeDtypeStruct((n, d), _bf16),
        scratch_shapes=[pltpu.VMEM((tq, width), _f32), pltpu.VMEM((tq, width), _f32)],
        compiler_params=_params("parallel", "parallel", "arbitrary"),
        name="sb_attention",
    )(q, k, v)


def kernel(x, a_norm, a_w_in, a_conv, a_log, a_dt_bias, a_out_norm, a_w_out, kv_norm, w_kv, k_norm, b_norm, b_w_q, q_norm, b_w_out, ffn_norm, ffn_w_up, ffn_conv, ffn_w_down):
    batch, seq, d = x.shape
    n = batch * seq
    n_a = a_w_in.shape[0]
    n_b = b_w_q.shape[0]
    gdn_heads = a_log.shape[1]
    bf = lambda w: w.astype(_bf16)
    xs = x.reshape(n, d)
    k_s = v_s = None
    for layer in range(n_a + n_b):
        if layer < n_a:
            proj, gb = gdn_in_proj(xs, a_norm[layer], a_w_in[layer], a_log[layer], a_dt_bias[layer])
            og = gdn_core(proj, gb, a_conv[layer], a_out_norm[layer], batch, seq, gdn_heads)
            xs = matmul_residual(og, bf(a_w_out[layer]), xs)
        else:
            j = layer - n_a
            if j == 0:
                k_s = norm_matmul(xs, kv_norm, bf(w_kv[:, :d]), _bf16, head_gain=k_norm)
                v_s = norm_matmul(xs, kv_norm, bf(w_kv[:, d:]), _bf16)
            q = norm_matmul(xs, b_norm[j], bf(b_w_q[j]), _bf16, head_gain=q_norm[j])
            o = sb_attention(q, k_s, v_s, batch, seq)
            xs = matmul_residual(o, bf(b_w_out[j]), xs)
        xs = conv_ffn(xs, ffn_norm[layer], bf(ffn_w_up[layer]), ffn_conv[layer], bf(ffn_w_down[layer]), seq)
    return xs.reshape(batch, seq, d)
```

```python
import functools

import jax
import jax.numpy as jnp
from jax import lax
from jax.experimental import pallas as pl
from jax.experimental.pallas import tpu as pltpu

EPS = 1e-6
HEAD_DIM = 128
GDN_CHUNK = 64
FFN_CONV_W = 3
F32_SUBLANES = 8
BF16_SUBLANES = 16
SB_KEY_BLOCK = 256
SB_STICK_EXHAUSTED = 104.0
VMEM_LIMIT_BYTES = 56 * 1024 * 1024

_f32 = jnp.float32
_bf16 = jnp.bfloat16


def _params(*semantics):
    return pltpu.CompilerParams(dimension_semantics=semantics, vmem_limit_bytes=VMEM_LIMIT_BYTES)


def _rms(x, gain):
    ms = jnp.mean(x * x, axis=-1, keepdims=True)
    return (x * lax.rsqrt(ms + EPS)) * gain


def _softplus(x):
    return jnp.maximum(x, 0.0) + jnp.log(1.0 + jnp.exp(-jnp.abs(x)))


def _sigmoid(x):
    return 1.0 / (1.0 + jnp.exp(-x))


def _dot(a, b):
    return jnp.dot(a, b, preferred_element_type=_f32)


def _dot_nt(a, b):
    return lax.dot_general(a, b, (((1,), (1,)), ((), ())), preferred_element_type=_f32)


def _dot_tn(a, b):
    return lax.dot_general(a, b, (((0,), (0,)), ((), ())), preferred_element_type=_f32)


def _split_bf16(x, pieces):
    out = []
    for _ in range(pieces):
        hi = x.astype(_bf16)
        out.append(hi)
        x = x - hi.astype(_f32)
    return out


def _norm_matmul_kernel(x_ref, g_ref, w_ref, *rest, head_norm):
    if head_norm:
        hg_ref, o_ref, xn_ref = rest
    else:
        o_ref, xn_ref = rest

    @pl.when(pl.program_id(1) == 0)
    def _():
        xn_ref[...] = _rms(x_ref[...], g_ref[...]).astype(xn_ref.dtype)

    y = _dot(xn_ref[...], w_ref[...])
    if head_norm:
        parts = []
        for h in range(y.shape[1] // HEAD_DIM):
            parts.append(_rms(y[:, h * HEAD_DIM:(h + 1) * HEAD_DIM], hg_ref[...]))
        y = jnp.concatenate(parts, axis=1)
    o_ref[...] = y.astype(o_ref.dtype)


def norm_matmul(x, gain, w, out_dtype, head_gain=None, tm=1024, tn=1024):
    n, d = x.shape
    nout = w.shape[1]
    tm, tn = min(tm, n), min(tn, nout)
    in_specs = [pl.BlockSpec((tm, d), lambda i, j: (i, 0)),
                pl.BlockSpec((1, d), lambda i, j: (0, 0)),
                pl.BlockSpec((d, tn), lambda i, j: (0, j))]
    args = [x, gain.reshape(1, d), w]
    if head_gain is not None:
        in_specs.append(pl.BlockSpec((1, HEAD_DIM), lambda i, j: (0, 0)))
        args.append(head_gain.reshape(1, HEAD_DIM))
    return pl.pallas_call(
        functools.partial(_norm_matmul_kernel, head_norm=head_gain is not None),
        grid=(n // tm, nout // tn),
        in_specs=in_specs,
        out_specs=pl.BlockSpec((tm, tn), lambda i, j: (i, j)),
        out_shape=jax.ShapeDtypeStruct((n, nout), out_dtype),
        scratch_shapes=[pltpu.VMEM((tm, d), _bf16)],
        compiler_params=_params("parallel", "arbitrary"),
        name="norm_matmul",
    )(*args)


RESIDUAL_RING = 3


def _matmul_residual_kernel(a_ref, w_ref, x_hbm, o_ref, ring_ref, sem, *, steps):
    i = pl.program_id(0)
    tm = o_ref.shape[0]

    def fetch(step):
        slot = step % RESIDUAL_RING
        rows = pl.ds(pl.multiple_of(step * tm, tm), tm)
        return pltpu.make_async_copy(x_hbm.at[rows, :], ring_ref.at[slot], sem.at[slot])

    @pl.when(i == 0)
    def _():
        for step in range(min(RESIDUAL_RING - 1, steps)):
            fetch(step).start()

    @pl.when(i + RESIDUAL_RING - 1 < steps)
    def _():
        fetch(i + RESIDUAL_RING - 1).start()

    fetch(i).wait()
    o_ref[...] = ring_ref[i % RESIDUAL_RING] + _dot(a_ref[...], w_ref[...])


def matmul_residual(a, w, x, tm=1024):
    n, k = a.shape
    d = w.shape[1]
    tm = min(tm, n)
    assert n % tm == 0, (n, tm)
    return pl.pallas_call(
        functools.partial(_matmul_residual_kernel, steps=n // tm),
        grid=(n // tm,),
        in_specs=[pl.BlockSpec((tm, k), lambda i: (i, 0)),
                  pl.BlockSpec((k, d), lambda i: (0, 0)),
                  pl.BlockSpec(memory_space=pl.ANY)],
        out_specs=pl.BlockSpec((tm, d), lambda i: (i, 0)),
        out_shape=jax.ShapeDtypeStruct((n, d), _f32),
        scratch_shapes=[pltpu.VMEM((RESIDUAL_RING, tm, d), _f32), pltpu.SemaphoreType.DMA((RESIDUAL_RING,))],
        compiler_params=_params("arbitrary"),
        name="matmul_residual",
    )(a, w, x)


def _gdn_in_proj_kernel(x_ref, g_ref, w_ref, wab_ref, alog_ref, dtb_ref, tri_ref, o_ref, gb_ref, xn_ref, *, heads):
    @pl.when(pl.program_id(1) == 0)
    def _():
        xn = _rms(x_ref[...], g_ref[...]).astype(_bf16)
        xn_ref[...] = xn
        y = _dot(xn, wab_ref[...])
        g = -jnp.exp(alog_ref[...]) * _softplus(y + dtb_ref[...])
        beta = _sigmoid(y)
        tri = tri_ref[...]
        rows = tri.shape[0]
        pieces = _split_bf16(g, 3)
        gc = jnp.concatenate([sum(_dot(tri, p[r:r + rows]) for p in pieces)
                              for r in range(0, y.shape[0], rows)], axis=0)
        lane = lax.broadcasted_iota(jnp.int32, y.shape, 1)
        gb_ref[...] = jnp.where(lane < heads, gc, beta)

    o_ref[...] = _dot(xn_ref[...], w_ref[...]).astype(o_ref.dtype)


def gdn_in_proj(x, gain, w_in, a_log, dt_bias, tm=1024, tn=2048, tri_rows=256):
    n, d = x.shape
    heads = a_log.shape[0]
    nout = w_in.shape[1] - 2 * heads
    tm, tn = min(tm, n), min(tn, nout)
    tri_rows = min(tri_rows, tm)
    assert n % tm == 0 and nout % tn == 0 and tm % tri_rows == 0 and tri_rows % GDN_CHUNK == 0
    pad = lambda v: jnp.zeros((1, HEAD_DIM), _f32).at[0, :heads].set(v.astype(_f32))
    w_ab = jnp.zeros((d, HEAD_DIM), _bf16).at[:, :2 * heads].set(w_in[:, nout:].astype(_bf16))
    r, c = jnp.arange(tri_rows)[:, None], jnp.arange(tri_rows)[None, :]
    tri = ((r // GDN_CHUNK == c // GDN_CHUNK) & (r >= c)).astype(_bf16)
    const = lambda shape: pl.BlockSpec(shape, lambda i, j: (0, 0))
    return pl.pallas_call(
        functools.partial(_gdn_in_proj_kernel, heads=heads),
        grid=(n // tm, nout // tn),
        in_specs=[pl.BlockSpec((tm, d), lambda i, j: (i, 0)), const((1, d)),
                  pl.BlockSpec((d, tn), lambda i, j: (0, j)),
                  const((d, HEAD_DIM)), const((1, HEAD_DIM)), const((1, HEAD_DIM)), const((tri_rows, tri_rows))],
        out_specs=[pl.BlockSpec((tm, tn), lambda i, j: (i, j)), pl.BlockSpec((tm, HEAD_DIM), lambda i, j: (i, 0))],
        out_shape=[jax.ShapeDtypeStruct((n, nout), _bf16), jax.ShapeDtypeStruct((n, HEAD_DIM), _f32)],
        scratch_shapes=[pltpu.VMEM((tm, d), _bf16)],
        compiler_params=_params("parallel", "arbitrary"),
        name="gdn_in_proj",
    )(x, gain.reshape(1, d), w_in[:, :nout].astype(_bf16), w_ab, pad(a_log), pad(dt_bias), tri)


def _bdot(a, b):
    return jnp.einsum("cij,cjk->cik", a, b, preferred_element_type=_f32)


def _bdot_nt(a, b):
    return jnp.einsum("cid,cjd->cij", a, b, preferred_element_type=_f32)


def _bdot3_stacked(lhs_list, b):
    n = lhs_list[0].shape[1]
    bh, bl = _split_bf16(b, 2)
    his, los = zip(*(_split_bf16(a, 2) for a in lhs_list))
    m = len(lhs_list)
    big = _bdot(jnp.concatenate(his + los, axis=1), bh)
    sml = _bdot(jnp.concatenate(his, axis=1), bl)
    rows = lambda t, i: t[:, i * n:(i + 1) * n]
    return [rows(big, i) + (rows(big, m + i) + rows(sml, i)) for i in range(m)]


def _unit_lower_inverse(low, eye):
    x = eye - low
    (p,) = _bdot3_stacked([low], low)
    for _ in range(4):
        xp, p = _bdot3_stacked([x, p], p)
        x = x + xp
    (xp,) = _bdot3_stacked([x], p)
    return x + xp


def _gdn_core_kernel(q_ref, k_ref, v_ref, z_ref, gb_ref, cq_ref, ck_ref, cv_ref, gain_ref, o_ref,
                     halo_ref, s_ref, *, heads, hp):
    tb = q_ref.shape[0]
    hd = HEAD_DIM
    cl = GDN_CHUNK
    nc = tb // cl
    h0 = pl.program_id(1) * hp

    @pl.when(pl.program_id(2) == 0)
    def _():
        halo_ref[...] = jnp.zeros_like(halo_ref)
        s_ref[...] = jnp.zeros_like(s_ref)

    def conv_silu(x_ref, w_ref, idx):
        raw = x_ref[...].astype(_f32)
        ext = jnp.concatenate([halo_ref[idx], raw], axis=0)
        w = w_ref[...]
        y = w[3:4] * raw
        for d in range(1, 4):
            y = y + w[3 - d:4 - d] * ext[F32_SUBLANES - d:F32_SUBLANES - d + tb]
        halo_ref[idx] = raw[tb - F32_SUBLANES:]
        return y * _sigmoid(y)

    def l2n(x):
        return x * lax.rsqrt(jnp.sum(x * x, axis=-1, keepdims=True) + EPS)

    qf = conv_silu(q_ref, cq_ref, 0)
    kf = conv_silu(k_ref, ck_ref, 1)
    vf = conv_silu(v_ref, cv_ref, 2)
    gb = gb_ref[...]
    lane = lax.broadcasted_iota(jnp.int32, gb.shape, 1)
    ri = lax.broadcasted_iota(jnp.int32, (cl, cl), 0)
    ci = lax.broadcasted_iota(jnp.int32, (cl, cl), 1)
    eye_mask, tril, strict = ri == ci, ri >= ci, ri > ci
    eye = eye_mask.astype(_f32)

    def per_head(fn):
        return jnp.concatenate([fn(hh, slice(hh * hd, (hh + 1) * hd)).reshape(nc, cl, hd)
                                for hh in range(hp)], axis=0)

    def gate_col(lane_of_head):
        return lambda hh, cols: jnp.broadcast_to(
            jnp.sum(jnp.where(lane == lane_of_head + hh, gb, 0.0), axis=-1, keepdims=True), (tb, hd))

    q = per_head(lambda hh, cols: l2n(qf[:, cols]) * (hd ** -0.5))
    k = per_head(lambda hh, cols: l2n(kf[:, cols]))
    v = per_head(lambda hh, cols: vf[:, cols])
    gc = per_head(gate_col(h0))
    bt = per_head(gate_col(h0 + heads))
    gc_sq = gc[:, :, :cl]
    grow = jnp.sum(jnp.where(eye_mask, gc_sq, 0.0), axis=1, keepdims=True)
    decay = jnp.where(tril, jnp.exp(gc_sq - grow), 0.0)
    eg = jnp.exp(gc)
    kb = k * bt
    k16 = k.astype(_bf16)
    kq = _bdot_nt(jnp.concatenate([kb.astype(_bf16), q.astype(_bf16)], axis=1), k16)
    low = jnp.where(strict, kq[:, :cl] * decay, 0.0)
    qk = jnp.where(tril, kq[:, cl:] * decay, 0.0).astype(_bf16)
    tinv = _unit_lower_inverse(low, eye)
    rhs = jnp.concatenate([v * bt, kb * eg], axis=2)
    (sol,) = _bdot3_stacked([tinv], rhs)
    u = sol[:, :, :hd]
    gl = gc[:, cl - 1:cl, :]
    egl = jnp.exp(gl)
    wq = jnp.concatenate([sol[:, :, hd:].astype(_bf16), (q * eg).astype(_bf16)], axis=1)
    kd_t = jnp.swapaxes(k * jnp.exp(gl - gc), 1, 2).astype(_bf16)
    qkd = jnp.concatenate([qk, kd_t], axis=1)

    state = [s_ref[hh] for hh in range(hp)]
    outs = [[] for _ in range(hp)]
    for c in range(nc):
        for hh in range(hp):
            b = hh * nc + c
            s = state[hh]
            ws_qs = _dot(wq[b], s.astype(_bf16))
            vn16 = (u[b] - ws_qs[:cl]).astype(_bf16)
            r = _dot(qkd[b], vn16)
            outs[hh].append(ws_qs[cl:] + r[:cl])
            state[hh] = s * egl[b] + r[cl:]

    for hh in range(hp):
        cols = slice(hh * hd, (hh + 1) * hd)
        s_ref[hh] = state[hh]
        z = z_ref[:, cols].astype(_f32)
        o = jnp.concatenate(outs[hh], axis=0)
        o_ref[:, cols] = (_rms(o, gain_ref[...]) * (z * _sigmoid(z))).astype(o_ref.dtype)


def gdn_core(proj, gb, conv_w, out_gain, batch, seq, heads, tb=256, hp=8):
    n = proj.shape[0]
    tb, hp = min(tb, seq), min(hp, heads)
    nt = seq // tb
    hg = heads // hp
    width = hp * HEAD_DIM
    row = lambda b, g, t: b * nt + t
    col = lambda off: (lambda b, g, t: (row(b, g, t), off * hg + g))
    ccol = lambda off: (lambda b, g, t: (0, off * hg + g))
    blk = (tb, width)
    return pl.pallas_call(
        functools.partial(_gdn_core_kernel, heads=heads, hp=hp),
        grid=(batch, hg, nt),
        in_specs=[pl.BlockSpec(blk, col(0)), pl.BlockSpec(blk, col(1)), pl.BlockSpec(blk, col(2)),
                  pl.BlockSpec(blk, col(3)),
                  pl.BlockSpec((tb, HEAD_DIM), lambda b, g, t: (row(b, g, t), 0)),
                  pl.BlockSpec((4, width), ccol(0)), pl.BlockSpec((4, width), ccol(1)),
                  pl.BlockSpec((4, width), ccol(2)),
                  pl.BlockSpec((1, HEAD_DIM), lambda b, g, t: (0, 0))],
        out_specs=pl.BlockSpec(blk, lambda b, g, t: (row(b, g, t), g)),
        out_shape=jax.ShapeDtypeStruct((n, heads * HEAD_DIM), _bf16),
        scratch_shapes=[pltpu.VMEM((3, F32_SUBLANES, width), _f32),
                        pltpu.VMEM((hp, HEAD_DIM, HEAD_DIM), _f32)],
        compiler_params=_params("parallel", "parallel", "arbitrary"),
        name="gdn_core",
    )(proj, proj, proj, proj, gb, conv_w, conv_w, conv_w, out_gain.reshape(1, HEAD_DIM))


def _ffn_kernel(x_ref, xh_ref, g_ref, wg_ref, wu_ref, cg_ref, cu_ref, wd_ref, o_ref, xn_ref,
                *, blocks_per_seq):
    tm = x_ref.shape[0]
    halo = BF16_SUBLANES
    i, f = pl.program_id(0), pl.program_id(1)

    @pl.when(f == 0)
    def _():
        xn_ref[halo:, :] = _rms(x_ref[...], g_ref[...]).astype(xn_ref.dtype)
        prev = jnp.where(i % blocks_per_seq == 0, 0.0, _rms(xh_ref[...], g_ref[...]))
        xn_ref[:halo, :] = prev.astype(xn_ref.dtype)
        o_ref[...] = x_ref[...]

    xn = xn_ref[...]

    def conv_branch(w_ref, c_ref):
        u = _dot(xn, w_ref[...])
        cw = c_ref[...]
        return (cw[2:3] * u[halo:] + cw[1:2] * u[halo - 1:halo - 1 + tm]
                + cw[0:1] * u[halo - 2:halo - 2 + tm])

    gate = conv_branch(wg_ref, cg_ref)
    up = conv_branch(wu_ref, cu_ref)
    act = ((gate * _sigmoid(gate)) * up).astype(_bf16)
    o_ref[...] += _dot(act, wd_ref[...])


def conv_ffn(x, gain, w_up, conv_w, w_down, seq, tm=1024, tf=256):
    n, d = x.shape
    ff = w_down.shape[0]
    tm, tf = min(tm, seq), min(tf, ff)
    assert seq % tm == 0 and ff % tf == 0 and tm % BF16_SUBLANES == 0, (seq, tm, ff, tf)
    nf = ff // tf
    hb = tm // BF16_SUBLANES
    return pl.pallas_call(
        functools.partial(_ffn_kernel, blocks_per_seq=seq // tm),
        grid=(n // tm, nf),
        in_specs=[pl.BlockSpec((tm, d), lambda i, f: (i, 0)),
                  pl.BlockSpec((BF16_SUBLANES, d), lambda i, f: (jnp.maximum(i * hb - 1, 0), 0)),
                  pl.BlockSpec((1, d), lambda i, f: (0, 0)),
                  pl.BlockSpec((d, tf), lambda i, f: (0, f)),
                  pl.BlockSpec((d, tf), lambda i, f: (0, nf + f)),
                  pl.BlockSpec((FFN_CONV_W, tf), lambda i, f: (0, f)),
                  pl.BlockSpec((FFN_CONV_W, tf), lambda i, f: (0, nf + f)),
                  pl.BlockSpec((tf, d), lambda i, f: (f, 0))],
        out_specs=pl.BlockSpec((tm, d), lambda i, f: (i, 0)),
        out_shape=jax.ShapeDtypeStruct((n, d), _f32),
        scratch_shapes=[pltpu.VMEM((BF16_SUBLANES + tm, d), _bf16)],
        compiler_params=_params("parallel", "arbitrary"),
        name="conv_ffn",
    )(x, x, gain.reshape(1, d), w_up, w_up, conv_w, conv_w, w_down)


def _sb_kernel(q_ref, k_ref, v_ref, o_ref, c_ref, acc_ref, *, scale, hp):
    tq = q_ref.shape[0]
    tk = SB_KEY_BLOCK
    hd = HEAD_DIM
    nsub = tq // tk
    i = pl.program_id(2)
    rr = lax.broadcasted_iota(jnp.int32, (tk, tk), 0)
    cc = lax.broadcasted_iota(jnp.int32, (tk, tk), 1)
    tri = (rr >= cc).astype(_bf16)
    c_ref[...] = jnp.zeros_like(c_ref)
    acc_ref[...] = jnp.zeros_like(acc_ref)

    def scores(row0, row1, kstart, masked, hh):
        cols = slice(hh * hd, (hh + 1) * hd)
        z = _dot_nt(q_ref[row0:row1, cols], k_ref[pl.ds(kstart, tk), cols]) * scale
        sp = _softplus(z)
        mask = None
        if masked:
            t_loc = lax.broadcasted_iota(jnp.int32, z.shape, 0)
            s_loc = lax.broadcasted_iota(jnp.int32, z.shape, 1)
            mask = s_loc < t_loc
            sp = jnp.where(mask, sp, 0.0)
        both = _dot(jnp.concatenate(_split_bf16(sp, 2), axis=0), tri)
        return z - (both[:row1 - row0] + both[row1 - row0:]), both[:row1 - row0, 0:1] + both[row1 - row0:, 0:1], mask

    def absorb(row0, row1, kstart, hh, z_minus_cum, total, mask):
        cols = slice(hh * hd, (hh + 1) * hd)
        c = c_ref[row0:row1, cols]
        a = jnp.exp(z_minus_cum - jnp.concatenate([c] * (tk // hd), axis=1))
        if mask is not None:
            a = jnp.where(mask, a, 0.0)
        acc_ref[row0:row1, cols] += _dot(a.astype(_bf16), v_ref[pl.ds(kstart, tk), cols])
        c_ref[row0:row1, cols] = c + total

    def process(tiles):
        pre = [[scores(*t, hh) for hh in range(hp)] for t in tiles]
        for t, per_head in zip(tiles, pre):
            for hh in range(hp):
                absorb(t[0], t[1], t[2], hh, *per_head[hh])

    process([(sub * tk, tq, pl.multiple_of(i * tq + sub * tk, tk), True) for sub in reversed(range(nsub))])

    def walk(row0, row1):
        def alive():
            return (jnp.min(c_ref[row0:row1, :]) <= SB_STICK_EXHAUSTED).astype(jnp.int32)

        def cond(carry):
            j, go = carry
            return jnp.logical_and(j >= 0, go > 0)

        def body(carry):
            j, _ = carry
            process([(row0, row1, pl.multiple_of(j * tk, tk), False)])
            return j - 1, alive()

        lax.while_loop(cond, body, (i * nsub - 1, alive()))

    groups = 2 if tq % (2 * BF16_SUBLANES) == 0 else 1
    for grp in range(groups):
        walk(grp * (tq // groups), (grp + 1) * (tq // groups))
    o_ref[...] = acc_ref[...].astype(o_ref.dtype)


def sb_attention(q, k, v, batch, seq, tq=512, hp=4):
    n, d = q.shape
    heads = d // HEAD_DIM
    tq, hp = min(tq, seq), min(hp, heads)
    nq = seq // tq
    width = hp * HEAD_DIM
    return pl.pallas_call(
        functools.partial(_sb_kernel, scale=HEAD_DIM ** -0.5, hp=hp),
        grid=(batch, heads // hp, nq),
        in_specs=[pl.BlockSpec((tq, width), lambda b, g, i: (b * nq + i, g)),
                  pl.BlockSpec((seq, width), lambda b, g, i: (b, g), pipeline_mode=pl.Buffered(1)),
                  pl.BlockSpec((seq, width), lambda b, g, i: (b, g), pipeline_mode=pl.Buffered(1))],
        out_specs=pl.BlockSpec((tq, width), lambda b, g, i: (b * nq + i, g)),
        out_shape=jax.ShapeDtypeStruct((n, d), _bf16),
        scratch_shapes=[pltpu.VMEM((tq, width), _f32), pltpu.VMEM((tq, width), _f32)],
        compiler_params=_params("parallel", "parallel", "arbitrary"),
        name="sb_attention",
    )(q, k, v)


def kernel(x, a_norm, a_w_in, a_conv, a_log, a_dt_bias, a_out_norm, a_w_out, kv_norm, w_kv, k_norm, b_norm, b_w_q, q_norm, b_w_out, ffn_norm, ffn_w_up, ffn_conv, ffn_w_down):
    batch, seq, d = x.shape
    n = batch * seq
    n_a = a_w_in.shape[0]
    n_b = b_w_q.shape[0]
    gdn_heads = a_log.shape[1]
    bf = lambda w: w.astype(_bf16)
    xs = x.reshape(n, d)
    k_s = v_s = None
    for layer in range(n_a + n_b):
        if layer < n_a:
            proj, gb = gdn_in_proj(xs, a_norm[layer], a_w_in[layer], a_log[layer], a_dt_bias[layer])
            og = gdn_core(proj, gb, a_conv[layer], a_out_norm[layer], batch, seq, gdn_heads)
            xs = matmul_residual(og, bf(a_w_out[layer]), xs)
        else:
            j = layer - n_a
            if j == 0:
                k_s = norm_matmul(xs, kv_norm, bf(w_kv[:, :d]), _bf16, head_gain=k_norm)
                v_s = norm_matmul(xs, kv_norm, bf(w_kv[:, d:]), _bf16)
            q = norm_matmul(xs, b_norm[j], bf(b_w_q[j]), _bf16, head_gain=q_norm[j])
            o = sb_attention(q, k_s, v_s, batch, seq)
            xs = matmul_residual(o, bf(b_w_out[j]), xs)
        xs = conv_ffn(xs, ffn_norm[layer], bf(ffn_w_up[layer]), ffn_conv[layer], bf(ffn_w_down[layer]), seq)
    return xs.reshape(batch, seq, d)
```
